```python
import math
import jax
import jax.numpy as jnp
from jax import lax
import numpy as np

D_MODEL = 1024
BATCH = 4
SEQ = 4096
DEPTH = 4
DEC_BATCH = 32
DEC_SEQ = 1
PAST_LEN = 8192
PAGE_SIZE = 128

N_MIXERS = 3
N_META = 16
ALPHA = (2 * DEPTH) ** 0.25
BETA_INIT = (8 * DEPTH) ** -0.25
D_FF = 2816
LN_EPS = 1e-5
RMS_EPS = 1e-6

GDN_HEADS = 8
GDN_DK = D_MODEL // GDN_HEADS
GDN_DV = D_MODEL // GDN_HEADS
GDN_CONV = 4
GDN_CHUNK = 64
GDN_QKV = GDN_HEADS * (2 * GDN_DK + GDN_DV)
GDN_Z = GDN_HEADS * GDN_DV
GDN_IN = GDN_QKV + GDN_Z + 2 * GDN_HEADS

SB_HEADS = 8
SB_DH = D_MODEL // SB_HEADS
SB_BLOCK = 128
SB_BIAS_INIT = -6.0

CONV_WIDTH = 31

N_A = (DEPTH + 2) // N_MIXERS
N_B = (DEPTH + 1) // N_MIXERS
N_C = DEPTH // N_MIXERS

kernel_name = 'hybrid_gdn_stickbreak_conformer_step'


def layer_norm(x, g, b):
    xf = x.astype(jnp.float32)
    mu = jnp.mean(xf, -1, keepdims=True)
    var = jnp.mean(jnp.square(xf - mu), -1, keepdims=True)
    y = (xf - mu) * lax.rsqrt(var + LN_EPS) * g.astype(jnp.float32) + b.astype(jnp.float32)
    return y.astype(x.dtype)


def residual_norm(h, delta, g, b):
    return layer_norm(ALPHA * h + delta, g, b)


def swiglu(x, w_gate, w_up, w_down):
    return (jax.nn.silu(x @ w_gate) * (x @ w_up)) @ w_down


def causal_dwconv(xp, w):
    return lax.conv_general_dilated(xp, w[:, None, :].astype(xp.dtype), window_strides=(1,), padding='VALID',
                                    dimension_numbers=('NWC', 'WIO', 'NWC'), feature_group_count=xp.shape[-1])


def l2norm(x):
    return x * lax.rsqrt(jnp.sum(x * x, -1, keepdims=True) + RMS_EPS)


def gdn_inputs(x, conv_ctx, w_in, conv_w, a_log, dt_bias):
    n, t, _ = x.shape
    h = x @ w_in
    qkv_pre = h[..., :GDN_QKV]
    z = h[..., GDN_QKV:GDN_QKV + GDN_Z].reshape(n, t, GDN_HEADS, GDN_DV)
    b_logit = h[..., GDN_QKV + GDN_Z:GDN_QKV + GDN_Z + GDN_HEADS].astype(jnp.float32)
    a_logit = h[..., GDN_QKV + GDN_Z + GDN_HEADS:].astype(jnp.float32)
    seq = jnp.concatenate([conv_ctx.astype(h.dtype), qkv_pre], axis=1)
    qkv = jax.nn.silu(causal_dwconv(seq, conv_w)).astype(jnp.float32)
    new_ctx = seq[:, -(GDN_CONV - 1):]
    hk = GDN_HEADS * GDN_DK
    q = l2norm(qkv[..., :hk].reshape(n, t, GDN_HEADS, GDN_DK)) * GDN_DK ** -0.5
    k = l2norm(qkv[..., hk:2 * hk].reshape(n, t, GDN_HEADS, GDN_DK))
    v = qkv[..., 2 * hk:].reshape(n, t, GDN_HEADS, GDN_DV)
    beta = jax.nn.sigmoid(b_logit)
    g = -jnp.exp(a_log.astype(jnp.float32)) * jax.nn.softplus(a_logit + dt_bias.astype(jnp.float32))
    sw = lambda u: jnp.swapaxes(u, 1, 2)
    return sw(q), sw(k), sw(v), sw(g), sw(beta), z, new_ctx


def gdn_chunk(S, q, k, v, g, beta):
    c = q.shape[2]
    gc = jnp.cumsum(g, axis=-1)
    lower = jnp.tril(jnp.ones((c, c), bool))
    decay = jnp.exp(jnp.where(lower, gc[..., :, None] - gc[..., None, :], -jnp.inf))
    kb = k * beta[..., None]
    a_mat = jnp.einsum('nhid,nhjd->nhij', kb, k) * decay
    rhs = jnp.concatenate([v * beta[..., None], kb * jnp.exp(gc)[..., None]], axis=-1)
    sol = lax.linalg.triangular_solve(a_mat, rhs, left_side=True, lower=True, unit_diagonal=True)
    u, w = sol[..., :GDN_DV], sol[..., GDN_DV:]
    v_new = u - jnp.einsum('nhik,nhkv->nhiv', w, S)
    qk = jnp.einsum('nhid,nhjd->nhij', q, k) * decay
    o = jnp.einsum('nhik,nhkv->nhiv', q * jnp.exp(gc)[..., None], S) + jnp.einsum('nhij,nhjv->nhiv', qk, v_new)
    g_last = gc[..., -1]
    S = S * jnp.exp(g_last)[..., None, None] + jnp.einsum(
        'nhik,nhiv->nhkv', k * jnp.exp(g_last[..., None] - gc)[..., None], v_new)
    return S, o


def gdn_prompt_core(q, k, v, g, beta):
    n, hh, t, _ = q.shape
    S = jnp.zeros((n, hh, GDN_DK, GDN_DV), jnp.float32)
    S, o_meta = gdn_chunk(S, q[:, :, :N_META], k[:, :, :N_META], v[:, :, :N_META], g[:, :, :N_META], beta[:, :, :N_META])
    nc = (t - N_META) // GDN_CHUNK

    def split(u):
        u = u[:, :, N_META:]
        return jnp.moveaxis(u.reshape(n, hh, nc, GDN_CHUNK, *u.shape[3:]), 2, 0)

    S, o_rest = lax.scan(lambda s, c: gdn_chunk(s, *c), S, tuple(split(u) for u in (q, k, v, g, beta)))
    o_rest = jnp.moveaxis(o_rest, 0, 2).reshape(n, hh, nc * GDN_CHUNK, GDN_DV)
    return jnp.concatenate([o_meta, o_rest], axis=2), S


def gdn_sample_core(S0, q, k, v, g, beta):
    def step(S, c):
        q_t, k_t, v_t, g_t, b_t = c
        S = S * jnp.exp(g_t)[..., None, None]
        delta = (v_t - jnp.einsum('nhk,nhkv->nhv', k_t, S)) * b_t[..., None]
        S = S + jnp.einsum('nhk,nhv->nhkv', k_t, delta)
        return S, jnp.einsum('nhk,nhkv->nhv', q_t, S)

    xs = tuple(jnp.moveaxis(u, 2, 0) for u in (q, k, v, g, beta))
    S, o = lax.scan(step, S0.astype(jnp.float32), xs)
    return jnp.moveaxis(o, 0, 2), S


def gdn_output(o, z, norm_w, w_out):
    o = jnp.swapaxes(o, 1, 2)
    o = o * lax.rsqrt(jnp.mean(o * o, -1, keepdims=True) + RMS_EPS) * norm_w.astype(jnp.float32)
    o = (o * jax.nn.silu(z.astype(jnp.float32))).astype(z.dtype)
    n, t = o.shape[:2]
    return o.reshape(n, t, GDN_Z) @ w_out


def sb_heads(x, w_qkv):
    n, t, _ = x.shape
    qkv = (x @ w_qkv).reshape(n, t, 3, SB_HEADS, SB_DH)
    return qkv[:, :, 0], qkv[:, :, 1], qkv[:, :, 2]


def sb_block(q, tq, k, v, tk, bias):
    z = jnp.einsum('nhqd,nhkd->nhqk', q, k, preferred_element_type=jnp.float32) * SB_DH ** -0.5
    z = z + bias.astype(jnp.float32)[None, :, None, None]
    vis = tk[None, :] < tq[:, None]
    log_fail = jnp.where(vis, jax.nn.log_sigmoid(-z), 0.0)
    after = lax.cumsum(log_fail, axis=3, reverse=True) - log_fail
    w = jnp.where(vis, jnp.exp(jax.nn.log_sigmoid(z) + after), 0.0)
    return jnp.einsum('nhqk,nhkd->nhqd', w.astype(v.dtype), v)


def sb_prompt(x, w_qkv, w_out, bias):
    n, t, _ = x.shape
    q, k, v = sb_heads(x, w_qkv)
    qh, kh, vh = jnp.swapaxes(q, 1, 2), jnp.swapaxes(k, 1, 2), jnp.swapaxes(v, 1, 2)
    pos = jnp.arange(t)
    o_meta = sb_block(qh[:, :, :N_META], pos[:N_META], kh[:, :, :N_META], vh[:, :, :N_META], pos[:N_META], bias)
    nb = (t - N_META) // SB_BLOCK
    q_blocks = jnp.moveaxis(qh[:, :, N_META:].reshape(n, SB_HEADS, nb, SB_BLOCK, SB_DH), 2, 0)
    p_blocks = pos[N_META:].reshape(nb, SB_BLOCK)
    o_rest = lax.map(lambda qp: sb_block(qp[0], qp[1], kh, vh, pos, bias), (q_blocks, p_blocks))
    o_rest = jnp.moveaxis(o_rest, 0, 2).reshape(n, SB_HEADS, t - N_META, SB_DH)
    o = jnp.swapaxes(jnp.concatenate([o_meta, o_rest], axis=2), 1, 2).reshape(n, t, D_MODEL)
    return o @ w_out, k, v


def sb_sample(x, cache_k, cache_v, page_table, w_qkv, w_out, bias):
    n, t, _ = x.shape
    q, k, v = sb_heads(x, w_qkv)
    past = page_table.shape[1] * PAGE_SIZE
    k_past = cache_k[page_table].reshape(n, past, SB_HEADS, SB_DH).astype(k.dtype)
    v_past = cache_v[page_table].reshape(n, past, SB_HEADS, SB_DH).astype(v.dtype)
    k_all = jnp.swapaxes(jnp.concatenate([k_past, k], axis=1), 1, 2)
    v_all = jnp.swapaxes(jnp.concatenate([v_past, v], axis=1), 1, 2)
    o = sb_block(jnp.swapaxes(q, 1, 2), past + jnp.arange(t), k_all, v_all, jnp.arange(past + t), bias)
    return jnp.swapaxes(o, 1, 2).reshape(n, t, D_MODEL) @ w_out, k, v


def conformer_conv(x, ctx, w_pw1, b_pw1, dw_w, dw_b, ln_g, ln_b, w_pw2, b_pw2):
    h = x @ w_pw1 + b_pw1
    u = h[..., :D_MODEL] * jax.nn.sigmoid(h[..., D_MODEL:])
    seq = jnp.concatenate([ctx.astype(u.dtype), u], axis=1)
    d = causal_dwconv(seq, dw_w) + dw_b
    d = jax.nn.silu(layer_norm(d, ln_g, ln_b))
    return d @ w_pw2 + b_pw2, seq[:, -(CONV_WIDTH - 1):]


def setup_inputs(seed: int = 0) -> dict:
    key = jax.random.key(seed)
    keys = jax.random.split(key, 40)
    it = iter(range(40))

    def nrm(shape, scale):
        return jax.random.normal(keys[next(it)], shape, jnp.float32) * scale

    n_pages = PAST_LEN // PAGE_SIZE
    n_phys = (5 * DEC_BATCH * n_pages) // 4
    dt = jnp.exp(jax.random.uniform(keys[next(it)], (N_A, GDN_HEADS), jnp.float32, math.log(1e-3), math.log(1e-1)))
    a_log = jnp.log(jax.random.uniform(keys[next(it)], (N_A, GDN_HEADS), jnp.float32, 1.0, 16.0))
    page_table = jax.random.permutation(keys[next(it)], n_phys)[:DEC_BATCH * n_pages]
    page_table = page_table.reshape(DEC_BATCH, n_pages).astype(jnp.int32)
    d_in = D_MODEL ** -0.5
    return {
        'x_prompt': nrm((BATCH, SEQ, D_MODEL), 1.0),
        'x_sample': nrm((DEC_BATCH, DEC_SEQ, D_MODEL), 1.0),
        'state_gdn_conv': nrm((N_A, DEC_BATCH, GDN_CONV - 1, GDN_QKV), 1.0),
        'state_gdn_S': nrm((N_A, DEC_BATCH, GDN_HEADS, GDN_DK, GDN_DV), 0.3),
        'cache_sb_k': nrm((N_B, n_phys, PAGE_SIZE, SB_HEADS, SB_DH), 1.0),
        'cache_sb_v': nrm((N_B, n_phys, PAGE_SIZE, SB_HEADS, SB_DH), 1.0),
        'state_conv': nrm((N_C, DEC_BATCH, CONV_WIDTH - 1, D_MODEL), 0.5),
        'page_table': page_table,
        'meta_tokens': nrm((N_META, D_MODEL), 1.0),
        'ln_g': 1.0 + nrm((DEPTH, 3, D_MODEL), 0.01),
        'ln_b': nrm((DEPTH, 3, D_MODEL), 0.01),
        'ffn_w_gate': nrm((DEPTH, 2, D_MODEL, D_FF), d_in),
        'ffn_w_up': nrm((DEPTH, 2, D_MODEL, D_FF), d_in),
        'ffn_w_down': nrm((DEPTH, 2, D_FF, D_MODEL), D_FF ** -0.5 * BETA_INIT),
        'gdn_w_in': nrm((N_A, D_MODEL, GDN_IN), d_in),
        'gdn_conv_w': nrm((N_A, GDN_CONV, GDN_QKV), GDN_CONV ** -0.5),
        'gdn_a_log': a_log,
        'gdn_dt_bias': dt + jnp.log(-jnp.expm1(-dt)),
        'gdn_norm_w': 1.0 + nrm((N_A, GDN_DV), 0.01),
        'gdn_w_out': nrm((N_A, GDN_Z, D_MODEL), GDN_Z ** -0.5 * BETA_INIT),
        'sb_w_qkv': nrm((N_B, D_MODEL, 3 * D_MODEL), d_in),
        'sb_w_out': nrm((N_B, D_MODEL, D_MODEL), d_in * BETA_INIT),
        'sb_logit_bias': SB_BIAS_INIT + nrm((N_B, SB_HEADS), 0.1),
        'cv_w_pw1': nrm((N_C, D_MODEL, 2 * D_MODEL), d_in),
        'cv_b_pw1': nrm((N_C, 2 * D_MODEL), 0.01),
        'cv_dw_w': nrm((N_C, CONV_WIDTH, D_MODEL), CONV_WIDTH ** -0.5),
        'cv_dw_b': nrm((N_C, D_MODEL), 0.01),
        'cv_ln_g': 1.0 + nrm((N_C, D_MODEL), 0.01),
        'cv_ln_b': nrm((N_C, D_MODEL), 0.01),
        'cv_w_pw2': nrm((N_C, D_MODEL, D_MODEL), d_in * BETA_INIT),
        'cv_b_pw2': nrm((N_C, D_MODEL), 0.01),
    }


def reference(x_prompt, x_sample, state_gdn_conv, state_gdn_S, cache_sb_k, cache_sb_v, state_conv, page_table,
              meta_tokens, ln_g, ln_b, ffn_w_gate, ffn_w_up, ffn_w_down,
              gdn_w_in, gdn_conv_w, gdn_a_log, gdn_dt_bias, gdn_norm_w, gdn_w_out,
              sb_w_qkv, sb_w_out, sb_logit_bias,
              cv_w_pw1, cv_b_pw1, cv_dw_w, cv_dw_b, cv_ln_g, cv_ln_b, cv_w_pw2, cv_b_pw2):
    n_p = x_prompt.shape[0]
    meta = jnp.broadcast_to(meta_tokens.astype(x_prompt.dtype)[None], (n_p, N_META, D_MODEL))
    hp = jnp.concatenate([meta, x_prompt], axis=1)
    hs = x_sample
    gdn_s_p, gdn_s_s, gdn_c_p, gdn_c_s = [], [], [], []
    sb_k_p, sb_v_p, sb_k_s, sb_v_s = [], [], [], []
    cv_p, cv_s = [], []

    for i in range(DEPTH):
        j, kind = i // N_MIXERS, i % N_MIXERS
        hp = residual_norm(hp, 0.5 * swiglu(hp, ffn_w_gate[i, 0], ffn_w_up[i, 0], ffn_w_down[i, 0]), ln_g[i, 0], ln_b[i, 0])
        hs = residual_norm(hs, 0.5 * swiglu(hs, ffn_w_gate[i, 0], ffn_w_up[i, 0], ffn_w_down[i, 0]), ln_g[i, 0], ln_b[i, 0])
        if kind == 0:
            pin = (gdn_w_in[j], gdn_conv_w[j], gdn_a_log[j], gdn_dt_bias[j])
            q, k, v, g, beta, z, ctx = gdn_inputs(hp, jnp.zeros((n_p, GDN_CONV - 1, GDN_QKV), hp.dtype), *pin)
            o, s_fin = gdn_prompt_core(q, k, v, g, beta)
            mix_p = gdn_output(o, z, gdn_norm_w[j], gdn_w_out[j])
            gdn_s_p.append(s_fin.astype(hp.dtype))
            gdn_c_p.append(ctx)
            q, k, v, g, beta, z, ctx = gdn_inputs(hs, state_gdn_conv[j], *pin)
            o, s_fin = gdn_sample_core(state_gdn_S[j], q, k, v, g, beta)
            mix_s = gdn_output(o, z, gdn_norm_w[j], gdn_w_out[j])
            gdn_s_s.append(s_fin.astype(state_gdn_S.dtype))
            gdn_c_s.append(ctx.astype(state_gdn_conv.dtype))
        elif kind == 1:
            mix_p, k, v = sb_prompt(hp, sb_w_qkv[j], sb_w_out[j], sb_logit_bias[j])
            sb_k_p.append(k)
            sb_v_p.append(v)
            mix_s, k, v = sb_sample(hs, cache_sb_k[j], cache_sb_v[j], page_table, sb_w_qkv[j], sb_w_out[j],
                                    sb_logit_bias[j])
            sb_k_s.append(k.astype(cache_sb_k.dtype))
            sb_v_s.append(v.astype(cache_sb_v.dtype))
        else:
            pc = (cv_w_pw1[j], cv_b_pw1[j], cv_dw_w[j], cv_dw_b[j], cv_ln_g[j], cv_ln_b[j], cv_w_pw2[j], cv_b_pw2[j])
            mix_p, ctx = conformer_conv(hp, jnp.zeros((n_p, CONV_WIDTH - 1, D_MODEL), hp.dtype), *pc)
            cv_p.append(ctx)
            mix_s, ctx = conformer_conv(hs, state_conv[j], *pc)
            cv_s.append(ctx.astype(state_conv.dtype))
        hp = residual_norm(hp, mix_p, ln_g[i, 1], ln_b[i, 1])
        hs = residual_norm(hs, mix_s, ln_g[i, 1], ln_b[i, 1])
        hp = residual_norm(hp, 0.5 * swiglu(hp, ffn_w_gate[i, 1], ffn_w_up[i, 1], ffn_w_down[i, 1]), ln_g[i, 2], ln_b[i, 2])
        hs = residual_norm(hs, 0.5 * swiglu(hs, ffn_w_gate[i, 1], ffn_w_up[i, 1], ffn_w_down[i, 1]), ln_g[i, 2], ln_b[i, 2])

    y_prompt = hp[:, N_META:]
    y_sample = hs
    return (y_prompt, y_sample,
            jnp.stack(gdn_s_p), jnp.stack(gdn_s_s), jnp.stack(gdn_c_p), jnp.stack(gdn_c_s),
            jnp.stack(sb_k_p), jnp.stack(sb_v_p), jnp.stack(sb_k_s), jnp.stack(sb_v_s),
            jnp.stack(cv_p), jnp.stack(cv_s))
```

```python
import functools
import math

import jax
import jax.numpy as jnp
from jax import lax
from jax.experimental import pallas as pl
from jax.experimental.pallas import tpu as pltpu

F32 = jnp.float32
BF16 = jnp.bfloat16

D_MODEL = 1024
DEPTH = 4
N_MIXERS = 3
N_META = 16
ALPHA = (2 * DEPTH) ** 0.25
D_FF = 2816
LN_EPS = 1e-5
RMS_EPS = 1e-6
PAGE_SIZE = 128

GDN_HEADS = 8
GDN_DK = 128
GDN_DV = 128
GDN_CONV = 4
GDN_QKV = GDN_HEADS * (2 * GDN_DK + GDN_DV)
GDN_Z = GDN_HEADS * GDN_DV
GDN_CHUNK = 64

SB_HEADS = 8
SB_DH = 128
SB_TQ = 384
SB_TK = 128

CONV_WIDTH = 31
CONV_HALO = 32

SEQ_ALIGN = 384
ROW_TILE = 512
SEQ_TILE = 384
LANES = 128
SUBLANES = 8
VMEM_LIMIT = 56 * 1024 * 1024


def _sigmoid(x):
    return 1.0 / (1.0 + jnp.exp(-x))


def _silu(x):
    return x * _sigmoid(x)


def _softplus(x):
    return jnp.maximum(x, 0.0) + jnp.log1p(jnp.exp(-jnp.abs(x)))


def _layer_norm(y, g, b):
    mu = jnp.mean(y, axis=-1, keepdims=True)
    yc = y - mu
    var = jnp.mean(yc * yc, axis=-1, keepdims=True)
    return yc * lax.rsqrt(var + LN_EPS) * g + b


def _dot(a, b):
    return jnp.dot(a.astype(BF16), b.astype(BF16), preferred_element_type=F32)


def _dot_nt(a, b):
    return lax.dot_general(a.astype(BF16), b.astype(BF16), (((1,), (1,)), ((), ())), preferred_element_type=F32)


def _dot_tn(a, b):
    return lax.dot_general(a.astype(BF16), b.astype(BF16), (((0,), (0,)), ((), ())), preferred_element_type=F32)


def _split(a):
    hi = a.astype(BF16)
    lo = (a - hi.astype(F32)).astype(BF16)
    return hi, lo


def _dot3(a, b):
    a_hi, a_lo = _split(a)
    b_hi, b_lo = _split(b)
    d = functools.partial(jnp.dot, preferred_element_type=F32)
    return d(a_hi, b_hi) + (d(a_hi, b_lo) + d(a_lo, b_hi))


def _dot2_exact_rhs(a, b_bf16):
    a_hi, a_lo = _split(a)
    d = functools.partial(jnp.dot, preferred_element_type=F32)
    return d(a_hi, b_bf16) + d(a_lo, b_bf16)


def _iota(shape, dim):
    return lax.broadcasted_iota(jnp.int32, shape, dim)


def _params(*sem):
    return pltpu.CompilerParams(dimension_semantics=sem, vmem_limit_bytes=VMEM_LIMIT)


def _const_spec(shape):
    nd = len(shape)
    return pl.BlockSpec(shape, lambda *_: (0,) * nd, pipeline_mode=pl.Buffered(1))


def _row_spec(tm, width):
    return pl.BlockSpec((tm, width), lambda i: (i, 0))


def _row_tile(rows):
    return ROW_TILE if rows % ROW_TILE == 0 else rows


FFN_SPLIT = 2


def _ffn_ln_kernel(x_ref, wg_ref, wu_ref, wd_ref, g_ref, b_ref, o_ref):
    x = x_ref[...]
    xb = x.astype(BF16)
    fc = D_FF // FFN_SPLIT
    acc = None
    for c in range(FFN_SPLIT):
        sl = slice(c * fc, (c + 1) * fc)
        gate = jnp.dot(xb, wg_ref[:, sl], preferred_element_type=F32)
        up = jnp.dot(xb, wu_ref[:, sl], preferred_element_type=F32)
        act = (_silu(gate) * up).astype(BF16)
        part = jnp.dot(act, wd_ref[sl, :], preferred_element_type=F32)
        acc = part if acc is None else acc + part
    o_ref[...] = _layer_norm(ALPHA * x + 0.5 * acc, g_ref[...], b_ref[...])


def ffn_ln(x, wg, wu, wd, g, b):
    rows = x.shape[0]
    tm = _row_tile(rows)
    return pl.pallas_call(
        _ffn_ln_kernel,
        grid=(rows // tm,),
        in_specs=[_row_spec(tm, D_MODEL), _const_spec(wg.shape), _const_spec(wu.shape), _const_spec(wd.shape),
                  _const_spec((1, D_MODEL)), _const_spec((1, D_MODEL))],
        out_specs=_row_spec(tm, D_MODEL),
        out_shape=jax.ShapeDtypeStruct((rows, D_MODEL), F32),
        compiler_params=_params("parallel"),
        name="ffn_ln",
    )(x, wg, wu, wd, g, b)


def _proj_kernel(x_ref, *refs, widths):
    n = len(widths)
    w_refs, o_refs = refs[:n], refs[n:]
    xb = x_ref[...].astype(BF16)
    for w_ref, o_ref, width in zip(w_refs, o_refs, widths):
        for c0 in range(0, width, D_MODEL):
            c1 = min(c0 + D_MODEL, width)
            o_ref[:, c0:c1] = jnp.dot(xb, w_ref[:, c0:c1], preferred_element_type=F32)


def proj(x, weights):
    rows = x.shape[0]
    tm = _row_tile(rows)
    widths = tuple(w.shape[1] for w in weights)
    outs = pl.pallas_call(
        functools.partial(_proj_kernel, widths=widths),
        grid=(rows // tm,),
        in_specs=[_row_spec(tm, D_MODEL)] + [_const_spec(w.shape) for w in weights],
        out_specs=[_row_spec(tm, n) for n in widths],
        out_shape=[jax.ShapeDtypeStruct((rows, n), F32) for n in widths],
        compiler_params=_params("parallel"),
        name="proj",
    )(x, *weights)
    return outs


def _proj_res_ln_kernel(a_ref, h_ref, w_ref, bias_ref, g_ref, b_ref, o_ref):
    y = jnp.dot(a_ref[...].astype(BF16), w_ref[...], preferred_element_type=F32) + bias_ref[...]
    o_ref[...] = _layer_norm(ALPHA * h_ref[...] + y, g_ref[...], b_ref[...])


def proj_res_ln(a, h, w, bias, g, b):
    rows = a.shape[0]
    tm = _row_tile(rows)
    return pl.pallas_call(
        _proj_res_ln_kernel,
        grid=(rows // tm,),
        in_specs=[_row_spec(tm, a.shape[1]), _row_spec(tm, D_MODEL), _const_spec(w.shape),
                  _const_spec((1, D_MODEL)), _const_spec((1, D_MODEL)), _const_spec((1, D_MODEL))],
        out_specs=_row_spec(tm, D_MODEL),
        out_shape=jax.ShapeDtypeStruct((rows, D_MODEL), F32),
        compiler_params=_params("parallel"),
        name="proj_res_ln",
    )(a, h, w, bias, g, b)


def _gdn_out_ln_kernel(o_ref, z_ref, h_ref, nw_ref, w_ref, g_ref, b_ref, out_ref):
    z = z_ref[...]
    gate = _silu(z)
    parts = []
    for hd in range(GDN_HEADS):
        sl = slice(hd * GDN_DV, (hd + 1) * GDN_DV)
        o = o_ref[:, sl]
        ms = jnp.mean(o * o, axis=-1, keepdims=True)
        parts.append((o * lax.rsqrt(ms + RMS_EPS) * nw_ref[...] * gate[:, sl]).astype(BF16))
    a = jnp.concatenate(parts, axis=-1)
    y = jnp.dot(a, w_ref[...], preferred_element_type=F32)
    out_ref[...] = _layer_norm(ALPHA * h_ref[...] + y, g_ref[...], b_ref[...])


def gdn_out_ln(o, z, h, norm_w, w_out, g, b):
    rows = o.shape[0]
    tm = _row_tile(rows)
    return pl.pallas_call(
        _gdn_out_ln_kernel,
        grid=(rows // tm,),
        in_specs=[_row_spec(tm, GDN_Z), _row_spec(tm, GDN_Z), _row_spec(tm, D_MODEL), _const_spec((1, GDN_DV)),
                  _const_spec(w_out.shape), _const_spec((1, D_MODEL)), _const_spec((1, D_MODEL))],
        out_specs=_row_spec(tm, D_MODEL),
        out_shape=jax.ShapeDtypeStruct((rows, D_MODEL), F32),
        compiler_params=_params("parallel"),
        name="gdn_out_ln",
    )(o, z, h, norm_w, w_out, g, b)


def _glu_kernel(x_ref, w_ref, bias_ref, o_ref):
    xb = x_ref[...].astype(BF16)
    a = jnp.dot(xb, w_ref[:, :D_MODEL], preferred_element_type=F32) + bias_ref[:, :D_MODEL]
    gate = jnp.dot(xb, w_ref[:, D_MODEL:], preferred_element_type=F32) + bias_ref[:, D_MODEL:]
    o_ref[...] = a * _sigmoid(gate)


def glu_proj(x, w, bias):
    rows = x.shape[0]
    tm = _row_tile(rows)
    return pl.pallas_call(
        _glu_kernel,
        grid=(rows // tm,),
        in_specs=[_row_spec(tm, D_MODEL), _const_spec(w.shape), _const_spec((1, 2 * D_MODEL))],
        out_specs=_row_spec(tm, D_MODEL),
        out_shape=jax.ShapeDtypeStruct((rows, D_MODEL), F32),
        compiler_params=_params("parallel"),
        name="glu_proj",
    )(x, w, bias)


def _halo_spec(ts, halo, width):
    per = ts // halo
    return pl.BlockSpec((None, halo, width), lambda b, t: (b, jnp.maximum(t * per - 1, 0), 0))


def _causal_taps(xx_ref, halo, rows, taps, w_ref, col):
    acc = None
    for k in range(taps):
        term = xx_ref[pl.ds(halo - k, rows), col] * w_ref[taps - 1 - k:taps - k, col]
        acc = term if acc is None else acc + term
    return acc


def _gdn_act_kernel(x_ref, halo_ref, ba_ref, cw_ref, alog_ref, dtb_ref, qkv_ref, bg_ref, xx_ref, *, ts, t_real):
    t = pl.program_id(1)
    halo = halo_ref[...]
    xx_ref[0:SUBLANES, :] = jnp.where(t == 0, 0.0, halo)
    xx_ref[SUBLANES:, :] = x_ref[...]
    hk = GDN_HEADS * GDN_DK
    for c in range(GDN_QKV // LANES):
        col = slice(c * LANES, (c + 1) * LANES)
        y = _silu(_causal_taps(xx_ref, SUBLANES, ts, GDN_CONV, cw_ref, col))
        if c * LANES < 2 * hk:
            y = y * lax.rsqrt(jnp.sum(y * y, axis=-1, keepdims=True) + RMS_EPS)
            if c * LANES < hk:
                y = y * GDN_DK ** -0.5
        qkv_ref[:, col] = y
    ba = ba_ref[...]
    row = t * ts + _iota(ba.shape, 0)
    lane = _iota(ba.shape, 1)
    beta = _sigmoid(ba)
    g = -jnp.exp(alog_ref[...]) * _softplus(ba + dtb_ref[...])
    bg_ref[...] = jnp.where(row < t_real, jnp.where(lane < GDN_HEADS, beta, g), 0.0)


def gdn_act(qkv_pre, ba, conv_w, a_log_row, dt_bias_row, t_real):
    nb, t_pad, _ = qkv_pre.shape
    ts = SEQ_TILE
    kern = functools.partial(_gdn_act_kernel, ts=ts, t_real=t_real)
    seq_spec = lambda width: pl.BlockSpec((None, ts, width), lambda b, t: (b, t, 0))
    return pl.pallas_call(
        kern,
        grid=(nb, t_pad // ts),
        in_specs=[seq_spec(GDN_QKV), _halo_spec(ts, SUBLANES, GDN_QKV), seq_spec(LANES),
                  _const_spec(conv_w.shape), _const_spec((1, LANES)), _const_spec((1, LANES))],
        out_specs=[seq_spec(GDN_QKV), seq_spec(LANES)],
        out_shape=[jax.ShapeDtypeStruct((nb, t_pad, GDN_QKV), F32), jax.ShapeDtypeStruct((nb, t_pad, LANES), F32)],
        scratch_shapes=[pltpu.VMEM((ts + SUBLANES, GDN_QKV), F32)],
        compiler_params=_params("parallel", "parallel"),
        name="gdn_act",
    )(qkv_pre, qkv_pre, ba, conv_w, a_log_row, dt_bias_row)


def _cv_conv_kernel(u_ref, halo_ref, w_ref, wb_ref, g_ref, b_ref, o_ref, xx_ref, *, ts):
    t = pl.program_id(1)
    xx_ref[0:CONV_HALO, :] = jnp.where(t == 0, 0.0, halo_ref[...])
    xx_ref[CONV_HALO:, :] = u_ref[...]
    d = _causal_taps(xx_ref, CONV_HALO, ts, CONV_WIDTH, w_ref, slice(None)) + wb_ref[...]
    o_ref[...] = _silu(_layer_norm(d, g_ref[...], b_ref[...]))


def cv_conv(u, dw_w, dw_b, ln_g, ln_b):
    nb, t_pad, _ = u.shape
    ts = SEQ_TILE
    seq_spec = pl.BlockSpec((None, ts, D_MODEL), lambda b, t: (b, t, 0))
    return pl.pallas_call(
        functools.partial(_cv_conv_kernel, ts=ts),
        grid=(nb, t_pad // ts),
        in_specs=[seq_spec, _halo_spec(ts, CONV_HALO, D_MODEL), _const_spec(dw_w.shape),
                  _const_spec((1, D_MODEL)), _const_spec((1, D_MODEL)), _const_spec((1, D_MODEL))],
        out_specs=seq_spec,
        out_shape=jax.ShapeDtypeStruct((nb, t_pad, D_MODEL), F32),
        scratch_shapes=[pltpu.VMEM((ts + CONV_HALO, D_MODEL), F32)],
        compiler_params=_params("parallel", "parallel"),
        name="cv_conv",
    )(u, u, dw_w, dw_b, ln_g, ln_b)


def _gdn_act_sample_kernel(seq_ref, ba_ref, cw_ref, alog_ref, dtb_ref, qkv_ref, bg_ref):
    hk = GDN_HEADS * GDN_DK
    for c in range(GDN_QKV // LANES):
        col = slice(c * LANES, (c + 1) * LANES)
        acc = None
        for i in range(GDN_CONV):
            term = seq_ref[i, :, col] * cw_ref[i:i + 1, col]
            acc = term if acc is None else acc + term
        y = _silu(acc)
        if c * LANES < 2 * hk:
            y = y * lax.rsqrt(jnp.sum(y * y, axis=-1, keepdims=True) + RMS_EPS)
            if c * LANES < hk:
                y = y * GDN_DK ** -0.5
        qkv_ref[:, col] = y
    ba = ba_ref[...]
    lane = _iota(ba.shape, 1)
    g = -jnp.exp(alog_ref[...]) * _softplus(ba + dtb_ref[...])
    bg_ref[...] = jnp.where(lane < GDN_HEADS, _sigmoid(ba), g)


def gdn_act_sample(seq, ba, conv_w, a_log_row, dt_bias_row):
    n = seq.shape[1]
    return pl.pallas_call(
        _gdn_act_sample_kernel,
        out_shape=[jax.ShapeDtypeStruct((n, GDN_QKV), F32), jax.ShapeDtypeStruct((n, LANES), F32)],
        compiler_params=pltpu.CompilerParams(vmem_limit_bytes=VMEM_LIMIT),
        name="gdn_act_sample",
    )(seq, ba, conv_w, a_log_row, dt_bias_row)


def _cv_conv_sample_kernel(seq_ref, w_ref, wb_ref, g_ref, b_ref, o_ref):
    acc = None
    for i in range(CONV_WIDTH):
        term = seq_ref[i] * w_ref[i:i + 1, :]
        acc = term if acc is None else acc + term
    o_ref[...] = _silu(_layer_norm(acc + wb_ref[...], g_ref[...], b_ref[...]))


def cv_conv_sample(seq, dw_w, dw_b, ln_g, ln_b):
    n = seq.shape[1]
    return pl.pallas_call(
        _cv_conv_sample_kernel,
        out_shape=jax.ShapeDtypeStruct((n, D_MODEL), F32),
        compiler_params=pltpu.CompilerParams(vmem_limit_bytes=VMEM_LIMIT),
        name="cv_conv_sample",
    )(seq, dw_w, dw_b, ln_g, ln_b)


def _gdn_chunk_kernel(qkv_ref, bg_ref, o_ref, s_out_ref, s_ref):
    c = pl.program_id(1)
    n_c = pl.num_programs(1)
    cs = GDN_CHUNK

    @pl.when(c == 0)
    def _():
        s_ref[...] = jnp.zeros_like(s_ref)

    bg = bg_ref[...]
    row = _iota((cs, LANES), 0)
    gc = bg
    shift = 1
    while shift < cs:
        gc = gc + jnp.where(row >= shift, pltpu.roll(gc, shift, axis=0), 0.0)
        shift *= 2
    gc_t = gc.T

    ri = _iota((cs, cs), 0)
    ci = _iota((cs, cs), 1)
    lower = ri >= ci
    strict = ri > ci
    eye = (ri == ci).astype(F32)

    for hd in range(GDN_HEADS):
        q = qkv_ref[:, hd * GDN_DK:(hd + 1) * GDN_DK]
        k = qkv_ref[:, GDN_HEADS * GDN_DK + hd * GDN_DK:GDN_HEADS * GDN_DK + (hd + 1) * GDN_DK]
        v = qkv_ref[:, 2 * GDN_HEADS * GDN_DK + hd * GDN_DV:2 * GDN_HEADS * GDN_DK + (hd + 1) * GDN_DV]
        beta = bg[:, hd:hd + 1]
        g_col = gc[:, GDN_HEADS + hd:GDN_HEADS + hd + 1]
        g_row = gc_t[GDN_HEADS + hd:GDN_HEADS + hd + 1, :]
        g_last = gc[cs - 1:cs, GDN_HEADS + hd:GDN_HEADS + hd + 1]
        decay = jnp.exp(jnp.where(lower, g_col - g_row, -jnp.inf))
        eg = jnp.exp(g_col)

        kb = k * beta
        a_mat = jnp.where(strict, _dot_nt(kb, k) * decay, 0.0)
        p = -a_mat
        t_inv = eye + p
        span = 2
        while span < cs:
            p = _dot3(p, p)
            t_inv = t_inv + _dot3(t_inv, p)
            span *= 2
        u = _dot3(t_inv, v * beta)
        w = _dot3(t_inv, kb * eg)

        s = s_ref[hd]
        v_new = u - _dot(w, s)
        qk = _dot_nt(q, k) * decay
        o_ref[:, hd * GDN_DV:(hd + 1) * GDN_DV] = _dot(q * eg, s) + _dot(qk, v_new)
        s_ref[hd] = s * jnp.exp(g_last) + _dot_tn(k * jnp.exp(g_last - g_col), v_new)

    @pl.when(c == n_c - 1)
    def _():
        s_out_ref[...] = s_ref[...]


def gdn_chunks(qkv, bg):
    nb, t_pad, _ = qkv.shape
    cs = GDN_CHUNK
    return pl.pallas_call(
        _gdn_chunk_kernel,
        grid=(nb, t_pad // cs),
        in_specs=[pl.BlockSpec((None, cs, GDN_QKV), lambda b, c: (b, c, 0)),
                  pl.BlockSpec((None, cs, LANES), lambda b, c: (b, c, 0))],
        out_specs=[pl.BlockSpec((None, cs, GDN_Z), lambda b, c: (b, c, 0)),
                   pl.BlockSpec((None, GDN_HEADS, GDN_DK, GDN_DV), lambda b, c: (b, 0, 0, 0))],
        out_shape=[jax.ShapeDtypeStruct((nb, t_pad, GDN_Z), F32),
                   jax.ShapeDtypeStruct((nb, GDN_HEADS, GDN_DK, GDN_DV), F32)],
        scratch_shapes=[pltpu.VMEM((GDN_HEADS, GDN_DK, GDN_DV), F32)],
        compiler_params=_params("parallel", "arbitrary"),
        name="gdn_chunks",
    )(qkv, bg)


def _gdn_step_kernel(s_ref, qc_ref, kc_ref, v_ref, bg_ref, o_ref, s_out_ref):
    for hd in range(GDN_HEADS):
        beta = bg_ref[:, hd:hd + 1]
        g = bg_ref[:, GDN_HEADS + hd:GDN_HEADS + hd + 1]
        k_col = kc_ref[hd]
        q_col = qc_ref[hd]
        v_row = v_ref[:, hd * GDN_DV:(hd + 1) * GDN_DV]
        s = s_ref[hd] * jnp.exp(g)
        ks = jnp.sum(s * k_col, axis=0, keepdims=True)
        delta = (v_row - ks) * beta
        s = s + k_col * delta
        s_out_ref[hd] = s
        o_ref[:, hd * GDN_DV:(hd + 1) * GDN_DV] = jnp.sum(s * q_col, axis=0, keepdims=True)


def gdn_step(s0, q_col, k_col, v, bg):
    n = s0.shape[0]
    s_spec = pl.BlockSpec((None, GDN_HEADS, GDN_DK, GDN_DV), lambda i: (i, 0, 0, 0))
    col_spec = pl.BlockSpec((None, GDN_HEADS, GDN_DK, 1), lambda i: (i, 0, 0, 0))
    return pl.pallas_call(
        _gdn_step_kernel,
        grid=(n,),
        in_specs=[s_spec, col_spec, col_spec,
                  pl.BlockSpec((None, 1, GDN_Z), lambda i: (i, 0, 0)),
                  pl.BlockSpec((None, 1, LANES), lambda i: (i, 0, 0))],
        out_specs=[pl.BlockSpec((None, 1, GDN_Z), lambda i: (i, 0, 0)), s_spec],
        out_shape=[jax.ShapeDtypeStruct((n, 1, GDN_Z), F32),
                   jax.ShapeDtypeStruct((n, GDN_HEADS, GDN_DK, GDN_DV), F32)],
        compiler_params=_params("parallel"),
        name="gdn_step",
    )(s0, q_col, k_col, v, bg)


def _sb_prompt_kernel(bias_ref, q_ref, k_ref, v_ref, o_ref):
    hd = pl.program_id(1)
    qi = pl.program_id(2)
    tq, tk = SB_TQ, SB_TK
    bias = bias_ref[hd]
    q = q_ref[...].astype(BF16)
    qpos = qi * tq + _iota((tq, tk), 0)
    kin = _iota((tq, tk), 1)
    suffix = (_iota((tk, tk), 0) > _iota((tk, tk), 1)).astype(BF16)
    n_kb = (qi + 1) * (tq // tk)

    def body(i, carry):
        acc, run = carry
        kb = n_kb - 1 - i
        k0 = pl.multiple_of(kb * tk, tk)
        kblk = k_ref[pl.ds(k0, tk), :].astype(BF16)
        vblk = v_ref[pl.ds(k0, tk), :].astype(BF16)
        z = lax.dot_general(q, kblk, (((1,), (1,)), ((), ())), preferred_element_type=F32) * SB_DH ** -0.5 + bias
        vis = (k0 + kin) < qpos
        sp = _softplus(z)
        log_fail = jnp.where(vis, -sp, 0.0)
        after = _dot2_exact_rhs(log_fail, suffix) + run
        w = jnp.where(vis, jnp.exp(z - sp + after), 0.0)
        acc = acc + jnp.dot(w.astype(BF16), vblk, preferred_element_type=F32)
        run = run + jnp.sum(log_fail, axis=-1, keepdims=True)
        return acc, run

    acc, _ = lax.fori_loop(0, n_kb, body, (jnp.zeros((tq, SB_DH), F32), jnp.zeros((tq, 1), F32)))
    o_ref[...] = acc


def sb_prompt_attn(qkv, bias):
    nb, t_pad, _ = qkv.shape
    tq = SB_TQ
    grid_spec = pltpu.PrefetchScalarGridSpec(
        num_scalar_prefetch=0,
        grid=(nb, SB_HEADS, t_pad // tq),
        in_specs=[pl.BlockSpec(memory_space=pltpu.SMEM),
                  pl.BlockSpec((None, tq, SB_DH), lambda b, h, i: (b, i, h)),
                  pl.BlockSpec((None, t_pad, SB_DH), lambda b, h, i: (b, 0, SB_HEADS + h)),
                  pl.BlockSpec((None, t_pad, SB_DH), lambda b, h, i: (b, 0, 2 * SB_HEADS + h))],
        out_specs=pl.BlockSpec((None, tq, SB_DH), lambda b, h, i: (b, i, h)),
    )
    return pl.pallas_call(
        _sb_prompt_kernel,
        grid_spec=grid_spec,
        out_shape=jax.ShapeDtypeStruct((nb, t_pad, D_MODEL), F32),
        compiler_params=_params("parallel", "parallel", "arbitrary"),
        name="sb_prompt_attn",
    )(bias, qkv, qkv, qkv)


def _sb_decode_kernel(pt_ref, q_ref, bias_ref, expand_ref, k_ref, v_ref, o_ref, acc_ref, run_ref):
    p = pl.program_id(1)
    n_p = pl.num_programs(1)
    ps = PAGE_SIZE

    @pl.when(p == 0)
    def _():
        acc_ref[...] = jnp.zeros_like(acc_ref)
        run_ref[...] = jnp.zeros_like(run_ref)

    z = jnp.dot(k_ref[...].astype(BF16), q_ref[...], preferred_element_type=F32) * SB_DH ** -0.5 + bias_ref[...]
    sp = _softplus(z)
    log_fail = -sp
    prefix = (_iota((ps, ps), 1) > _iota((ps, ps), 0)).astype(BF16)
    lf_hi, lf_lo = _split(log_fail)
    after = (jnp.dot(prefix, lf_hi, preferred_element_type=F32) + jnp.dot(prefix, lf_lo, preferred_element_type=F32)
             + run_ref[...])
    w = jnp.exp(z - sp + after)
    w_wide = jnp.dot(w.astype(BF16), expand_ref[...], preferred_element_type=F32)
    prod = w_wide * v_ref[...]
    acc_ref[...] += jnp.sum(prod.reshape(ps // SUBLANES, SUBLANES, D_MODEL), axis=0)
    run_ref[...] += jnp.sum(log_fail, axis=0, keepdims=True)

    @pl.when(p == n_p - 1)
    def _():
        o_ref[...] = jnp.sum(acc_ref[...], axis=0, keepdims=True)


def sb_decode_attn(q_blk, bias_row, expand, cache_k, cache_v, page_table):
    n, n_pages = page_table.shape
    page_spec = pl.BlockSpec((None, PAGE_SIZE, D_MODEL), lambda s, p, pt: (pt[s, n_pages - 1 - p], 0, 0))
    grid_spec = pltpu.PrefetchScalarGridSpec(
        num_scalar_prefetch=1,
        grid=(n, n_pages),
        in_specs=[pl.BlockSpec((None, D_MODEL, LANES), lambda s, p, pt: (s, 0, 0)),
                  pl.BlockSpec((1, LANES), lambda s, p, pt: (0, 0)),
                  pl.BlockSpec((LANES, D_MODEL), lambda s, p, pt: (0, 0)),
                  page_spec, page_spec],
        out_specs=pl.BlockSpec((None, 1, D_MODEL), lambda s, p, pt: (s, 0, 0)),
        scratch_shapes=[pltpu.VMEM((SUBLANES, D_MODEL), F32), pltpu.VMEM((1, LANES), F32)],
    )
    return pl.pallas_call(
        _sb_decode_kernel,
        grid_spec=grid_spec,
        out_shape=jax.ShapeDtypeStruct((n, 1, D_MODEL), F32),
        compiler_params=_params("parallel", "arbitrary"),
        name="sb_decode_attn",
    )(page_table, q_blk, bias_row, expand, cache_k, cache_v)


def _row(v):
    return v.reshape(1, -1).astype(F32)


def _pad_lanes(v, offset):
    out = jnp.zeros((1, LANES), F32)
    return lax.dynamic_update_slice(out, v.reshape(1, -1).astype(F32), (0, offset))


def kernel(x_prompt, x_sample, state_gdn_conv, state_gdn_S, cache_sb_k, cache_sb_v, state_conv, page_table,
           meta_tokens, ln_g, ln_b, ffn_w_gate, ffn_w_up, ffn_w_down,
           gdn_w_in, gdn_conv_w, gdn_a_log, gdn_dt_bias, gdn_norm_w, gdn_w_out,
           sb_w_qkv, sb_w_out, sb_logit_bias,
           cv_w_pw1, cv_b_pw1, cv_dw_w, cv_dw_b, cv_ln_g, cv_ln_b, cv_w_pw2, cv_b_pw2):
    nb, seq, _ = x_prompt.shape
    ns = x_sample.shape[0]
    assert x_sample.shape[1] == 1, "the sample group holds one new token per sequence"
    t_real = N_META + seq
    t_pad = -(-t_real // SEQ_ALIGN) * SEQ_ALIGN
    rows_p = nb * t_pad

    meta = jnp.broadcast_to(meta_tokens.astype(F32)[None], (nb, N_META, D_MODEL))
    hp = jnp.concatenate([meta, x_prompt, jnp.zeros((nb, t_pad - t_real, D_MODEL), F32)], axis=1)
    hp = hp.reshape(rows_p, D_MODEL)
    hs = x_sample.reshape(ns, D_MODEL)

    gdn_s_p, gdn_s_s, gdn_c_p, gdn_c_s = [], [], [], []
    sb_k_p, sb_v_p, sb_k_s, sb_v_s = [], [], [], []
    cv_p, cv_s = [], []
    zero_bias = jnp.zeros((1, D_MODEL), F32)

    def ffn(h, i, half, ln_idx):
        return ffn_ln(h, ffn_w_gate[i, half].astype(BF16), ffn_w_up[i, half].astype(BF16),
                      ffn_w_down[i, half].astype(BF16), _row(ln_g[i, ln_idx]), _row(ln_b[i, ln_idx]))

    for i in range(DEPTH):
        j, kind = i // N_MIXERS, i % N_MIXERS
        hp = ffn(hp, i, 0, 0)
        hs = ffn(hs, i, 0, 0)
        g1, b1 = _row(ln_g[i, 1]), _row(ln_b[i, 1])

        if kind == 0:
            w_in = gdn_w_in[j]
            w_qkv = w_in[:, :GDN_QKV].astype(BF16)
            w_z = w_in[:, GDN_QKV:GDN_QKV + GDN_Z].astype(BF16)
            w_ba = jnp.pad(w_in[:, GDN_QKV + GDN_Z:], ((0, 0), (0, LANES - 2 * GDN_HEADS))).astype(BF16)
            conv_w = gdn_conv_w[j].astype(F32)
            a_log_row = _pad_lanes(gdn_a_log[j], GDN_HEADS)
            dt_bias_row = _pad_lanes(gdn_dt_bias[j], GDN_HEADS)
            norm_w = _row(gdn_norm_w[j])
            w_out = gdn_w_out[j].astype(BF16)

            qkv_pre, z, ba = proj(hp, [w_qkv, w_z, w_ba])
            qkv_pre3 = qkv_pre.reshape(nb, t_pad, GDN_QKV)
            qkv_act, bg = gdn_act(qkv_pre3, ba.reshape(nb, t_pad, LANES), conv_w, a_log_row, dt_bias_row, t_real)
            o, s_fin = gdn_chunks(qkv_act, bg)
            hp = gdn_out_ln(o.reshape(rows_p, GDN_Z), z, hp, norm_w, w_out, g1, b1)
            gdn_s_p.append(s_fin)
            gdn_c_p.append(qkv_pre3[:, t_real - (GDN_CONV - 1):t_real])

            qkv_pre, z, ba = proj(hs, [w_qkv, w_z, w_ba])
            seq_s = jnp.concatenate([state_gdn_conv[j].astype(F32), qkv_pre[:, None]], axis=1)
            qkv_act, bg = gdn_act_sample(jnp.swapaxes(seq_s, 0, 1), ba, conv_w, a_log_row, dt_bias_row)
            hk = GDN_HEADS * GDN_DK
            q_col = qkv_act[:, :hk].reshape(ns, GDN_HEADS, GDN_DK, 1)
            k_col = qkv_act[:, hk:2 * hk].reshape(ns, GDN_HEADS, GDN_DK, 1)
            v_row = qkv_act[:, 2 * hk:].reshape(ns, 1, GDN_Z)
            o, s_fin = gdn_step(state_gdn_S[j].astype(F32), q_col, k_col, v_row, bg.reshape(ns, 1, LANES))
            hs = gdn_out_ln(o.reshape(ns, GDN_Z), z, hs, norm_w, w_out, g1, b1)
            gdn_s_s.append(s_fin)
            gdn_c_s.append(seq_s[:, 1:])

        elif kind == 1:
            w_qkv = sb_w_qkv[j].astype(BF16)
            w_out = sb_w_out[j].astype(BF16)
            bias = sb_logit_bias[j].astype(F32)

            (qkv,) = proj(hp, [w_qkv])
            qkv3 = qkv.reshape(nb, t_pad, 3 * D_MODEL)
            o = sb_prompt_attn(qkv3, bias)
            hp = proj_res_ln(o.reshape(rows_p, D_MODEL), hp, w_out, zero_bias, g1, b1)
            sb_k_p.append(qkv3[:, :t_real, D_MODEL:2 * D_MODEL].reshape(nb, t_real, SB_HEADS, SB_DH))
            sb_v_p.append(qkv3[:, :t_real, 2 * D_MODEL:].reshape(nb, t_real, SB_HEADS, SB_DH))

            (qkv,) = proj(hs, [w_qkv])
            q = qkv[:, :D_MODEL]
            head_of_row = jnp.arange(D_MODEL) // SB_DH
            blk = (head_of_row[:, None] == jnp.arange(LANES)[None, :])
            q_blk = jnp.where(blk[None], q[:, :, None], 0.0).astype(BF16)
            expand = (jnp.arange(LANES)[:, None] == head_of_row[None, :]).astype(BF16)
            bias_row = _pad_lanes(bias, 0)
            n_phys = cache_sb_k.shape[1]
            o = sb_decode_attn(q_blk, bias_row, expand,
                               cache_sb_k.reshape(-1, PAGE_SIZE, D_MODEL).astype(F32),
                               cache_sb_v.reshape(-1, PAGE_SIZE, D_MODEL).astype(F32),
                               page_table.astype(jnp.int32) + j * n_phys)
            hs = proj_res_ln(o.reshape(ns, D_MODEL), hs, w_out, zero_bias, g1, b1)
            sb_k_s.append(qkv[:, D_MODEL:2 * D_MODEL].reshape(ns, 1, SB_HEADS, SB_DH))
            sb_v_s.append(qkv[:, 2 * D_MODEL:].reshape(ns, 1, SB_HEADS, SB_DH))

        else:
            w1 = cv_w_pw1[j].astype(BF16)
            b_pw1 = _row(cv_b_pw1[j])
            dw_w = cv_dw_w[j].astype(F32)
            dw_b, cg, cb = _row(cv_dw_b[j]), _row(cv_ln_g[j]), _row(cv_ln_b[j])
            w2 = cv_w_pw2[j].astype(BF16)
            b_pw2 = _row(cv_b_pw2[j])

            u = glu_proj(hp, w1, b_pw1).reshape(nb, t_pad, D_MODEL)
            d = cv_conv(u, dw_w, dw_b, cg, cb)
            hp = proj_res_ln(d.reshape(rows_p, D_MODEL), hp, w2, b_pw2, g1, b1)
            cv_p.append(u[:, t_real - (CONV_WIDTH - 1):t_real])

            u = glu_proj(hs, w1, b_pw1)
            seq_s = jnp.concatenate([state_conv[j].astype(F32), u[:, None]], axis=1)
            d = cv_conv_sample(jnp.swapaxes(seq_s, 0, 1), dw_w, dw_b, cg, cb)
            hs = proj_res_ln(d, hs, w2, b_pw2, g1, b1)
            cv_s.append(seq_s[:, 1:])

        hp = ffn(hp, i, 1, 2)
        hs = ffn(hs, i, 1, 2)

    y_prompt = hp.reshape(nb, t_pad, D_MODEL)[:, N_META:t_real]
    y_sample = hs.reshape(ns, 1, D_MODEL)
    return (y_prompt, y_sample,
            jnp.stack(gdn_s_p), jnp.stack(gdn_s_s), jnp.stack(gdn_c_p), jnp.stack(gdn_c_s),
            jnp.stack(sb_k_p), jnp.stack(sb_v_p), jnp.stack(sb_k_s), jnp.stack(sb_v_s),
            jnp.stack(cv_p), jnp.stack(cv_s))
```

```python
import functools
import math

import jax
import jax.numpy as jnp
from jax import lax
from jax.experimental import pallas as pl
from jax.experimental.pallas import tpu as pltpu

F32 = jnp.float32
BF16 = jnp.bfloat16

D_MODEL = 1024
DEPTH = 4
N_MIXERS = 3
N_META = 16
ALPHA = (2 * DEPTH) ** 0.25
D_FF = 2816
LN_EPS = 1e-5
RMS_EPS = 1e-6
PAGE_SIZE = 128

GDN_HEADS = 8
GDN_DK = 128
GDN_DV = 128
GDN_CONV = 4
GDN_QKV = GDN_HEADS * (2 * GDN_DK + GDN_DV)
GDN_Z = GDN_HEADS * GDN_DV
GDN_CHUNK = 64

SB_HEADS = 8
SB_DH = 128
SB_TQ = 384
SB_TK = 128

CONV_WIDTH = 31
CONV_HALO = 32

SEQ_ALIGN = 384
ROW_TILE = 512
SEQ_TILE = 384
LANES = 128
SUBLANES = 8
VMEM_LIMIT = 56 * 1024 * 1024


def _sigmoid(x):
    return 1.0 / (1.0 + jnp.exp(-x))


def _silu(x):
    return x * _sigmoid(x)


def _softplus(x):
    return jnp.maximum(x, 0.0) + jnp.log1p(jnp.exp(-jnp.abs(x)))


def _layer_norm(y, g, b):
    mu = jnp.mean(y, axis=-1, keepdims=True)
    yc = y - mu
    var = jnp.mean(yc * yc, axis=-1, keepdims=True)
    return yc * lax.rsqrt(var + LN_EPS) * g + b


def _dot(a, b):
    return jnp.dot(a.astype(BF16), b.astype(BF16), preferred_element_type=F32)


def _dot_nt(a, b):
    return lax.dot_general(a.astype(BF16), b.astype(BF16), (((1,), (1,)), ((), ())), preferred_element_type=F32)


def _dot_tn(a, b):
    return lax.dot_general(a.astype(BF16), b.astype(BF16), (((0,), (0,)), ((), ())), preferred_element_type=F32)


def _split(a):
    hi = a.astype(BF16)
    lo = (a - hi.astype(F32)).astype(BF16)
    return hi, lo


def _dot3(a, b):
    a_hi, a_lo = _split(a)
    b_hi, b_lo = _split(b)
    d = functools.partial(jnp.dot, preferred_element_type=F32)
    return d(a_hi, b_hi) + (d(a_hi, b_lo) + d(a_lo, b_hi))


def _dot2_exact_rhs(a, b_bf16):
    a_hi, a_lo = _split(a)
    d = functools.partial(jnp.dot, preferred_element_type=F32)
    return d(a_hi, b_bf16) + d(a_lo, b_bf16)


def _iota(shape, dim):
    return lax.broadcasted_iota(jnp.int32, shape, dim)


def _params(*sem):
    return pltpu.CompilerParams(dimension_semantics=sem, vmem_limit_bytes=VMEM_LIMIT)


def _const_spec(shape):
    nd = len(shape)
    return pl.BlockSpec(shape, lambda *_: (0,) * nd, pipeline_mode=pl.Buffered(1))


def _row_spec(tm, width):
    return pl.BlockSpec((tm, width), lambda i: (i, 0))


def _row_tile(rows):
    return ROW_TILE if rows % ROW_TILE == 0 else rows


FFN_SPLIT = 2


def _ffn_ln_kernel(x_ref, wg_ref, wu_ref, wd_ref, g_ref, b_ref, o_ref):
    x = x_ref[...]
    xb = x.astype(BF16)
    fc = D_FF // FFN_SPLIT
    acc = None
    for c in range(FFN_SPLIT):
        sl = slice(c * fc, (c + 1) * fc)
        gate = jnp.dot(xb, wg_ref[:, sl], preferred_element_type=F32)
        up = jnp.dot(xb, wu_ref[:, sl], preferred_element_type=F32)
        act = (_silu(gate) * up).astype(BF16)
        part = jnp.dot(act, wd_ref[sl, :], preferred_element_type=F32)
        acc = part if acc is None else acc + part
    o_ref[...] = _layer_norm(ALPHA * x + 0.5 * acc, g_ref[...], b_ref[...])


def ffn_ln(x, wg, wu, wd, g, b):
    rows = x.shape[0]
    tm = _row_tile(rows)
    return pl.pallas_call(
        _ffn_ln_kernel,
        grid=(rows // tm,),
        in_specs=[_row_spec(tm, D_MODEL), _const_spec(wg.shape), _const_spec(wu.shape), _const_spec(wd.shape),
                  _const_spec((1, D_MODEL)), _const_spec((1, D_MODEL))],
        out_specs=_row_spec(tm, D_MODEL),
        out_shape=jax.ShapeDtypeStruct((rows, D_MODEL), F32),
        compiler_params=_params("parallel"),
        name="ffn_ln",
    )(x, wg, wu, wd, g, b)


def _proj_kernel(x_ref, *refs, widths):
    n = len(widths)
    w_refs, o_refs = refs[:n], refs[n:]
    xb = x_ref[...].astype(BF16)
    for w_ref, o_ref, width in zip(w_refs, o_refs, widths):
        for c0 in range(0, width, D_MODEL):
            c1 = min(c0 + D_MODEL, width)
            o_ref[:, c0:c1] = jnp.dot(xb, w_ref[:, c0:c1], preferred_element_type=F32)


def proj(x, weights):
    rows = x.shape[0]
    tm = _row_tile(rows)
    widths = tuple(w.shape[1] for w in weights)
    outs = pl.pallas_call(
        functools.partial(_proj_kernel, widths=widths),
        grid=(rows // tm,),
        in_specs=[_row_spec(tm, D_MODEL)] + [_const_spec(w.shape) for w in weights],
        out_specs=[_row_spec(tm, n) for n in widths],
        out_shape=[jax.ShapeDtypeStruct((rows, n), F32) for n in widths],
        compiler_params=_params("parallel"),
        name="proj",
    )(x, *weights)
    return outs


def _proj_res_ln_kernel(a_ref, h_ref, w_ref, bias_ref, g_ref, b_ref, o_ref):
    y = jnp.dot(a_ref[...].astype(BF16), w_ref[...], preferred_element_type=F32) + bias_ref[...]
    o_ref[...] = _layer_norm(ALPHA * h_ref[...] + y, g_ref[...], b_ref[...])


def proj_res_ln(a, h, w, bias, g, b):
    rows = a.shape[0]
    tm = _row_tile(rows)
    return pl.pallas_call(
        _proj_res_ln_kernel,
        grid=(rows // tm,),
        in_specs=[_row_spec(tm, a.shape[1]), _row_spec(tm, D_MODEL), _const_spec(w.shape),
                  _const_spec((1, D_MODEL)), _const_spec((1, D_MODEL)), _const_spec((1, D_MODEL))],
        out_specs=_row_spec(tm, D_MODEL),
        out_shape=jax.ShapeDtypeStruct((rows, D_MODEL), F32),
        compiler_params=_params("parallel"),
        name="proj_res_ln",
    )(a, h, w, bias, g, b)


def _gdn_out_ln_kernel(o_ref, z_ref, h_ref, nw_ref, w_ref, g_ref, b_ref, out_ref):
    z = z_ref[...]
    gate = _silu(z)
    parts = []
    for hd in range(GDN_HEADS):
        sl = slice(hd * GDN_DV, (hd + 1) * GDN_DV)
        o = o_ref[:, sl]
        ms = jnp.mean(o * o, axis=-1, keepdims=True)
        parts.append((o * lax.rsqrt(ms + RMS_EPS) * nw_ref[...] * gate[:, sl]).astype(BF16))
    a = jnp.concatenate(parts, axis=-1)
    y = jnp.dot(a, w_ref[...], preferred_element_type=F32)
    out_ref[...] = _layer_norm(ALPHA * h_ref[...] + y, g_ref[...], b_ref[...])


def gdn_out_ln(o, z, h, norm_w, w_out, g, b):
    rows = o.shape[0]
    tm = _row_tile(rows)
    return pl.pallas_call(
        _gdn_out_ln_kernel,
        grid=(rows // tm,),
        in_specs=[_row_spec(tm, GDN_Z), _row_spec(tm, GDN_Z), _row_spec(tm, D_MODEL), _const_spec((1, GDN_DV)),
                  _const_spec(w_out.shape), _const_spec((1, D_MODEL)), _const_spec((1, D_MODEL))],
        out_specs=_row_spec(tm, D_MODEL),
        out_shape=jax.ShapeDtypeStruct((rows, D_MODEL), F32),
        compiler_params=_params("parallel"),
        name="gdn_out_ln",
    )(o, z, h, norm_w, w_out, g, b)


def _glu_kernel(x_ref, w_ref, bias_ref, o_ref):
    xb = x_ref[...].astype(BF16)
    a = jnp.dot(xb, w_ref[:, :D_MODEL], preferred_element_type=F32) + bias_ref[:, :D_MODEL]
    gate = jnp.dot(xb, w_ref[:, D_MODEL:], preferred_element_type=F32) + bias_ref[:, D_MODEL:]
    o_ref[...] = a * _sigmoid(gate)


def glu_proj(x, w, bias):
    rows = x.shape[0]
    tm = _row_tile(rows)
    return pl.pallas_call(
        _glu_kernel,
        grid=(rows // tm,),
        in_specs=[_row_spec(tm, D_MODEL), _const_spec(w.shape), _const_spec((1, 2 * D_MODEL))],
        out_specs=_row_spec(tm, D_MODEL),
        out_shape=jax.ShapeDtypeStruct((rows, D_MODEL), F32),
        compiler_params=_params("parallel"),
        name="glu_proj",
    )(x, w, bias)


def _halo_spec(ts, halo, width):
    per = ts // halo
    return pl.BlockSpec((None, halo, width), lambda b, t: (b, jnp.maximum(t * per - 1, 0), 0))


def _causal_taps(xx_ref, halo, rows, taps, w_ref, col):
    acc = None
    for k in range(taps):
        term = xx_ref[pl.ds(halo - k, rows), col] * w_ref[taps - 1 - k:taps - k, col]
        acc = term if acc is None else acc + term
    return acc


def _gdn_act_kernel(x_ref, halo_ref, ba_ref, cw_ref, alog_ref, dtb_ref, qkv_ref, bg_ref, xx_ref, *, ts, t_real):
    t = pl.program_id(1)
    halo = halo_ref[...]
    xx_ref[0:SUBLANES, :] = jnp.where(t == 0, 0.0, halo)
    xx_ref[SUBLANES:, :] = x_ref[...]
    hk = GDN_HEADS * GDN_DK
    for c in range(GDN_QKV // LANES):
        col = slice(c * LANES, (c + 1) * LANES)
        y = _silu(_causal_taps(xx_ref, SUBLANES, ts, GDN_CONV, cw_ref, col))
        if c * LANES < 2 * hk:
            y = y * lax.rsqrt(jnp.sum(y * y, axis=-1, keepdims=True) + RMS_EPS)
            if c * LANES < hk:
                y = y * GDN_DK ** -0.5
        qkv_ref[:, col] = y
    ba = ba_ref[...]
    row = t * ts + _iota(ba.shape, 0)
    lane = _iota(ba.shape, 1)
    beta = _sigmoid(ba)
    g = -jnp.exp(alog_ref[...]) * _softplus(ba + dtb_ref[...])
    bg_ref[...] = jnp.where(row < t_real, jnp.where(lane < GDN_HEADS, beta, g), 0.0)


def gdn_act(qkv_pre, ba, conv_w, a_log_row, dt_bias_row, t_real):
    nb, t_pad, _ = qkv_pre.shape
    ts = SEQ_TILE
    kern = functools.partial(_gdn_act_kernel, ts=ts, t_real=t_real)
    seq_spec = lambda width: pl.BlockSpec((None, ts, width), lambda b, t: (b, t, 0))
    return pl.pallas_call(
        kern,
        grid=(nb, t_pad // ts),
        in_specs=[seq_spec(GDN_QKV), _halo_spec(ts, SUBLANES, GDN_QKV), seq_spec(LANES),
                  _const_spec(conv_w.shape), _const_spec((1, LANES)), _const_spec((1, LANES))],
        out_specs=[seq_spec(GDN_QKV), seq_spec(LANES)],
        out_shape=[jax.ShapeDtypeStruct((nb, t_pad, GDN_QKV), F32), jax.ShapeDtypeStruct((nb, t_pad, LANES), F32)],
        scratch_shapes=[pltpu.VMEM((ts + SUBLANES, GDN_QKV), F32)],
        compiler_params=_params("parallel", "parallel"),
        name="gdn_act",
    )(qkv_pre, qkv_pre, ba, conv_w, a_log_row, dt_bias_row)


def _cv_conv_kernel(u_ref, halo_ref, w_ref, wb_ref, g_ref, b_ref, o_ref, xx_ref, *, ts):
    t = pl.program_id(1)
    xx_ref[0:CONV_HALO, :] = jnp.where(t == 0, 0.0, halo_ref[...])
    xx_ref[CONV_HALO:, :] = u_ref[...]
    d = _causal_taps(xx_ref, CONV_HALO, ts, CONV_WIDTH, w_ref, slice(None)) + wb_ref[...]
    o_ref[...] = _silu(_layer_norm(d, g_ref[...], b_ref[...]))


def cv_conv(u, dw_w, dw_b, ln_g, ln_b):
    nb, t_pad, _ = u.shape
    ts = SEQ_TILE
    seq_spec = pl.BlockSpec((None, ts, D_MODEL), lambda b, t: (b, t, 0))
    return pl.pallas_call(
        functools.partial(_cv_conv_kernel, ts=ts),
        grid=(nb, t_pad // ts),
        in_specs=[seq_spec, _halo_spec(ts, CONV_HALO, D_MODEL), _const_spec(dw_w.shape),
                  _const_spec((1, D_MODEL)), _const_spec((1, D_MODEL)), _const_spec((1, D_MODEL))],
        out_specs=seq_spec,
        out_shape=jax.ShapeDtypeStruct((nb, t_pad, D_MODEL), F32),
        scratch_shapes=[pltpu.VMEM((ts + CONV_HALO, D_MODEL), F32)],
        compiler_params=_params("parallel", "parallel"),
        name="cv_conv",
    )(u, u, dw_w, dw_b, ln_g, ln_b)


def _gdn_act_sample_kernel(seq_ref, ba_ref, cw_ref, alog_ref, dtb_ref, qkv_ref, bg_ref):
    hk = GDN_HEADS * GDN_DK
    for c in range(GDN_QKV // LANES):
        col = slice(c * LANES, (c + 1) * LANES)
        acc = None
        for i in range(GDN_CONV):
            term = seq_ref[i, :, col] * cw_ref[i:i + 1, col]
            acc = term if acc is None else acc + term
        y = _silu(acc)
        if c * LANES < 2 * hk:
            y = y * lax.rsqrt(jnp.sum(y * y, axis=-1, keepdims=True) + RMS_EPS)
            if c * LANES < hk:
                y = y * GDN_DK ** -0.5
        qkv_ref[:, col] = y
    ba = ba_ref[...]
    lane = _iota(ba.shape, 1)
    g = -jnp.exp(alog_ref[...]) * _softplus(ba + dtb_ref[...])
    bg_ref[...] = jnp.where(lane < GDN_HEADS, _sigmoid(ba), g)


def gdn_act_sample(seq, ba, conv_w, a_log_row, dt_bias_row):
    n = seq.shape[1]
    return pl.pallas_call(
        _gdn_act_sample_kernel,
        out_shape=[jax.ShapeDtypeStruct((n, GDN_QKV), F32), jax.ShapeDtypeStruct((n, LANES), F32)],
        compiler_params=pltpu.CompilerParams(vmem_limit_bytes=VMEM_LIMIT),
        name="gdn_act_sample",
    )(seq, ba, conv_w, a_log_row, dt_bias_row)


def _cv_conv_sample_kernel(seq_ref, w_ref, wb_ref, g_ref, b_ref, o_ref):
    acc = None
    for i in range(CONV_WIDTH):
        term = seq_ref[i] * w_ref[i:i + 1, :]
        acc = term if acc is None else acc + term
    o_ref[...] = _silu(_layer_norm(acc + wb_ref[...], g_ref[...], b_ref[...]))


def cv_conv_sample(seq, dw_w, dw_b, ln_g, ln_b):
    n = seq.shape[1]
    return pl.pallas_call(
        _cv_conv_sample_kernel,
        out_shape=jax.ShapeDtypeStruct((n, D_MODEL), F32),
        compiler_params=pltpu.CompilerParams(vmem_limit_bytes=VMEM_LIMIT),
        name="cv_conv_sample",
    )(seq, dw_w, dw_b, ln_g, ln_b)


def _gdn_chunk_kernel(qkv_ref, bg_ref, o_ref, s_out_ref, s_ref):
    c = pl.program_id(1)
    n_c = pl.num_programs(1)
    cs = GDN_CHUNK

    @pl.when(c == 0)
    def _():
        s_ref[...] = jnp.zeros_like(s_ref)

    bg = bg_ref[...]
    row = _iota((cs, LANES), 0)
    gc = bg
    shift = 1
    while shift < cs:
        gc = gc + jnp.where(row >= shift, pltpu.roll(gc, shift, axis=0), 0.0)
        shift *= 2
    gc_t = gc.T

    ri = _iota((cs, cs), 0)
    ci = _iota((cs, cs), 1)
    lower = ri >= ci
    strict = ri > ci
    eye = (ri == ci).astype(F32)

    heads = range(GDN_HEADS)
    hk = GDN_HEADS * GDN_DK
    q = [qkv_ref[:, hd * GDN_DK:(hd + 1) * GDN_DK] for hd in heads]
    k = [qkv_ref[:, hk + hd * GDN_DK:hk + (hd + 1) * GDN_DK] for hd in heads]
    v = [qkv_ref[:, 2 * hk + hd * GDN_DV:2 * hk + (hd + 1) * GDN_DV] for hd in heads]
    beta = [bg[:, hd:hd + 1] for hd in heads]
    g_col = [gc[:, GDN_HEADS + hd:GDN_HEADS + hd + 1] for hd in heads]
    g_row = [gc_t[GDN_HEADS + hd:GDN_HEADS + hd + 1, :] for hd in heads]
    g_last = [gc[cs - 1:cs, GDN_HEADS + hd:GDN_HEADS + hd + 1] for hd in heads]
    decay = [jnp.exp(jnp.where(lower, g_col[hd] - g_row[hd], -jnp.inf)) for hd in heads]
    eg = [jnp.exp(g_col[hd]) for hd in heads]
    kb = [k[hd] * beta[hd] for hd in heads]

    kk = [_dot_nt(kb[hd], k[hd]) for hd in heads]
    qk = [_dot_nt(q[hd], k[hd]) for hd in heads]
    p = [-jnp.where(strict, kk[hd] * decay[hd], 0.0) for hd in heads]
    t_inv = [eye + p[hd] for hd in heads]
    span = 2
    while span < cs:
        p = [_dot3(p[hd], p[hd]) for hd in heads]
        t_inv = [t_inv[hd] + _dot3(t_inv[hd], p[hd]) for hd in heads]
        span *= 2
    u = [_dot3(t_inv[hd], v[hd] * beta[hd]) for hd in heads]
    w = [_dot3(t_inv[hd], kb[hd] * eg[hd]) for hd in heads]

    s = [s_ref[hd] for hd in heads]
    ws = [_dot(w[hd], s[hd]) for hd in heads]
    qs = [_dot(q[hd] * eg[hd], s[hd]) for hd in heads]
    v_new = [u[hd] - ws[hd] for hd in heads]
    o_in = [_dot(qk[hd] * decay[hd], v_new[hd]) for hd in heads]
    s_add = [_dot_tn(k[hd] * jnp.exp(g_last[hd] - g_col[hd]), v_new[hd]) for hd in heads]
    for hd in heads:
        o_ref[:, hd * GDN_DV:(hd + 1) * GDN_DV] = qs[hd] + o_in[hd]
        s_ref[hd] = s[hd] * jnp.exp(g_last[hd]) + s_add[hd]

    @pl.when(c == n_c - 1)
    def _():
        s_out_ref[...] = s_ref[...]


def gdn_chunks(qkv, bg):
    nb, t_pad, _ = qkv.shape
    cs = GDN_CHUNK
    return pl.pallas_call(
        _gdn_chunk_kernel,
        grid=(nb, t_pad // cs),
        in_specs=[pl.BlockSpec((None, cs, GDN_QKV), lambda b, c: (b, c, 0)),
                  pl.BlockSpec((None, cs, LANES), lambda b, c: (b, c, 0))],
        out_specs=[pl.BlockSpec((None, cs, GDN_Z), lambda b, c: (b, c, 0)),
                   pl.BlockSpec((None, GDN_HEADS, GDN_DK, GDN_DV), lambda b, c: (b, 0, 0, 0))],
        out_shape=[jax.ShapeDtypeStruct((nb, t_pad, GDN_Z), F32),
                   jax.ShapeDtypeStruct((nb, GDN_HEADS, GDN_DK, GDN_DV), F32)],
        scratch_shapes=[pltpu.VMEM((GDN_HEADS, GDN_DK, GDN_DV), F32)],
        compiler_params=_params("parallel", "arbitrary"),
        name="gdn_chunks",
    )(qkv, bg)


def _gdn_step_kernel(s_ref, qc_ref, kc_ref, v_ref, bg_ref, o_ref, s_out_ref):
    for hd in range(GDN_HEADS):
        beta = bg_ref[:, hd:hd + 1]
        g = bg_ref[:, GDN_HEADS + hd:GDN_HEADS + hd + 1]
        k_col = kc_ref[hd]
        q_col = qc_ref[hd]
        v_row = v_ref[:, hd * GDN_DV:(hd + 1) * GDN_DV]
        s = s_ref[hd] * jnp.exp(g)
        ks = jnp.sum(s * k_col, axis=0, keepdims=True)
        delta = (v_row - ks) * beta
        s = s + k_col * delta
        s_out_ref[hd] = s
        o_ref[:, hd * GDN_DV:(hd + 1) * GDN_DV] = jnp.sum(s * q_col, axis=0, keepdims=True)


def gdn_step(s0, q_col, k_col, v, bg):
    n = s0.shape[0]
    s_spec = pl.BlockSpec((None, GDN_HEADS, GDN_DK, GDN_DV), lambda i: (i, 0, 0, 0))
    col_spec = pl.BlockSpec((None, GDN_HEADS, GDN_DK, 1), lambda i: (i, 0, 0, 0))
    return pl.pallas_call(
        _gdn_step_kernel,
        grid=(n,),
        in_specs=[s_spec, col_spec, col_spec,
                  pl.BlockSpec((None, 1, GDN_Z), lambda i: (i, 0, 0)),
                  pl.BlockSpec((None, 1, LANES), lambda i: (i, 0, 0))],
        out_specs=[pl.BlockSpec((None, 1, GDN_Z), lambda i: (i, 0, 0)), s_spec],
        out_shape=[jax.ShapeDtypeStruct((n, 1, GDN_Z), F32),
                   jax.ShapeDtypeStruct((n, GDN_HEADS, GDN_DK, GDN_DV), F32)],
        compiler_params=_params("parallel"),
        name="gdn_step",
    )(s0, q_col, k_col, v, bg)


SB_HEAD_GROUP = 2


def _sb_prompt_kernel(bias_ref, q_ref, k_ref, v_ref, o_ref):
    hg = pl.program_id(1)
    qi = pl.program_id(2)
    tq, tk, dh = SB_TQ, SB_TK, SB_DH
    n_sub = tq // tk
    heads = range(SB_HEAD_GROUP)
    bias = [bias_ref[hg * SB_HEAD_GROUP + h] for h in heads]
    q = [(q_ref[:, h * dh:(h + 1) * dh] * dh ** -0.5).astype(BF16) for h in heads]
    tr = _iota((2 * tk, 2 * tk), 0) & (tk - 1)
    tc = _iota((2 * tk, 2 * tk), 1)
    tail = jnp.where((tc >= tk) | (tr > tc), -1.0, 0.0).astype(BF16)
    below_diag = _iota((tq, tq), 1) < _iota((tq, tq), 0)

    def tile(k0, masked, acc, run):
        kblk = [k_ref[pl.ds(k0, tq), h * dh:(h + 1) * dh].astype(BF16) for h in heads]
        vblk = [v_ref[pl.ds(k0, tq), h * dh:(h + 1) * dh].astype(BF16) for h in heads]
        z = [lax.dot_general(q[h], kblk[h], (((1,), (1,)), ((), ())), preferred_element_type=F32) + bias[h]
             for h in heads]
        z_pos = [jnp.maximum(z[h], 0.0) for h in heads]
        z_neg = [jnp.minimum(z[h], 0.0) for h in heads]
        lg = [jnp.log(1.0 + jnp.exp(z_neg[h] - z_pos[h])) for h in heads]
        sp = [z_pos[h] + lg[h] for h in heads]
        log_beta = [z_neg[h] - lg[h] for h in heads]
        if masked:
            sp = [jnp.where(below_diag, sp[h], 0.0) for h in heads]
        split = [_split(sp[h]) for h in heads]
        res = [[jnp.dot(jnp.concatenate([part[:, s * tk:(s + 1) * tk] for part in split[h]], axis=1),
                        tail, preferred_element_type=F32) for s in range(n_sub)] for h in heads]
        acc, run = list(acc), list(run)
        for h in heads:
            w = [None] * n_sub
            for s in reversed(range(n_sub)):
                w[s] = jnp.exp(log_beta[h][:, s * tk:(s + 1) * tk] + (res[h][s][:, :tk] + run[h]))
                run[h] = run[h] + res[h][s][:, tk:]
            w = jnp.concatenate(w, axis=1)
            if masked:
                w = jnp.where(below_diag, w, 0.0)
            acc[h] = acc[h] + jnp.dot(w.astype(BF16), vblk[h], preferred_element_type=F32)
        return tuple(acc), tuple(run)

    zeros = lambda width: tuple(jnp.zeros((tq, width), F32) for _ in heads)
    acc, run = tile(pl.multiple_of(qi * tq, tq), True, zeros(dh), zeros(tk))

    def body(i, carry):
        return tile(pl.multiple_of((qi - 1 - i) * tq, tq), False, *carry)

    acc, _ = lax.fori_loop(0, qi, body, (acc, run))
    for h in heads:
        o_ref[:, h * dh:(h + 1) * dh] = acc[h]


def sb_prompt_attn(qkv, bias):
    nb, t_pad, _ = qkv.shape
    tq = SB_TQ
    gw = SB_HEAD_GROUP * SB_DH
    n_groups = SB_HEADS // SB_HEAD_GROUP
    grid_spec = pltpu.PrefetchScalarGridSpec(
        num_scalar_prefetch=0,
        grid=(nb, n_groups, t_pad // tq),
        in_specs=[pl.BlockSpec(memory_space=pltpu.SMEM),
                  pl.BlockSpec((None, tq, gw), lambda b, g, i: (b, i, g)),
                  pl.BlockSpec((None, t_pad, gw), lambda b, g, i: (b, 0, n_groups + g)),
                  pl.BlockSpec((None, t_pad, gw), lambda b, g, i: (b, 0, 2 * n_groups + g))],
        out_specs=pl.BlockSpec((None, tq, gw), lambda b, g, i: (b, i, g)),
    )
    return pl.pallas_call(
        _sb_prompt_kernel,
        grid_spec=grid_spec,
        out_shape=jax.ShapeDtypeStruct((nb, t_pad, D_MODEL), F32),
        compiler_params=_params("parallel", "parallel", "arbitrary"),
        name="sb_prompt_attn",
    )(bias, qkv, qkv, qkv)


SB_PAGES_PER_STEP = 8


def _sb_decode_kernel(pt_ref, q_ref, bias_ref, expand_ref, *refs, n_pg):
    k_refs, v_refs = refs[:n_pg], refs[n_pg:2 * n_pg]
    o_ref, acc_ref, run_ref = refs[2 * n_pg:]
    step = pl.program_id(1)
    n_steps = pl.num_programs(1)
    ps = PAGE_SIZE
    pages = range(n_pg)

    @pl.when(step == 0)
    def _():
        acc_ref[...] = jnp.zeros_like(acc_ref)
        run_ref[...] = jnp.zeros_like(run_ref)

    q_mat = q_ref[...]
    lane = _iota((ps, LANES), 1)
    z = []
    for i in pages:
        zi = jnp.zeros((ps, LANES), F32)
        for hd in range(SB_HEADS):
            k_head = k_refs[i][pl.ds(hd, ps, stride=SB_HEADS), :]
            zh = jnp.dot(k_head.astype(BF16), q_mat, preferred_element_type=F32)
            zi = jnp.where(lane == hd, zh, zi)
        z.append(zi * SB_DH ** -0.5 + bias_ref[...])
    lg = [jnp.log(1.0 + jnp.exp(-jnp.abs(z[i]))) for i in pages]
    sp = [jnp.maximum(z[i], 0.0) + lg[i] for i in pages]
    log_beta = [jnp.minimum(z[i], 0.0) - lg[i] for i in pages]
    nr = _iota((ps, 2 * ps), 0)
    nc = _iota((ps, 2 * ps), 1) & (ps - 1)
    newer = jnp.where(nc > nr, -1.0, 0.0).astype(BF16)
    after = [jnp.dot(newer, jnp.concatenate(_split(sp[i]), axis=0), preferred_element_type=F32) for i in pages]
    run = run_ref[...]
    w = []
    for i in pages:
        w.append(jnp.exp(log_beta[i] + after[i] + run).astype(BF16))
        run = run + (after[i][0:1, :] - sp[i][0:1, :])
    run_ref[...] = run
    w_wide = [jnp.dot(w[i], expand_ref[...], preferred_element_type=F32) for i in pages]
    for hd in range(SB_HEADS):
        part = None
        for i in pages:
            v_head = v_refs[i][pl.ds(hd, ps, stride=SB_HEADS), :]
            prod = w_wide[i][:, hd * SB_DH:(hd + 1) * SB_DH] * v_head
            prod = jnp.sum(prod.reshape(ps // SUBLANES, SUBLANES, SB_DH), axis=0)
            part = prod if part is None else part + prod
        acc_ref[hd] += part

    @pl.when(step == n_steps - 1)
    def _():
        for hd in range(SB_HEADS):
            o_ref[:, hd * SB_DH:(hd + 1) * SB_DH] = jnp.sum(acc_ref[hd], axis=0, keepdims=True)


def sb_decode_attn(q_mat, bias_row, expand, cache_k, cache_v, page_table):
    n, n_pages = page_table.shape
    n_pg = math.gcd(n_pages, SB_PAGES_PER_STEP)

    def page_spec(i):
        return pl.BlockSpec((None, PAGE_SIZE * SB_HEADS, SB_DH),
                            lambda s, p, pt: (pt[s, n_pages - 1 - (p * n_pg + i)], 0, 0))

    grid_spec = pltpu.PrefetchScalarGridSpec(
        num_scalar_prefetch=1,
        grid=(n, n_pages // n_pg),
        in_specs=[pl.BlockSpec((None, SB_DH, LANES), lambda s, p, pt: (s, 0, 0)),
                  pl.BlockSpec((1, LANES), lambda s, p, pt: (0, 0)),
                  pl.BlockSpec((LANES, D_MODEL), lambda s, p, pt: (0, 0))]
                 + [page_spec(i) for i in range(n_pg)] * 2,
        out_specs=pl.BlockSpec((None, 1, D_MODEL), lambda s, p, pt: (s, 0, 0)),
        scratch_shapes=[pltpu.VMEM((SB_HEADS, SUBLANES, SB_DH), F32), pltpu.VMEM((1, LANES), F32)],
    )
    return pl.pallas_call(
        functools.partial(_sb_decode_kernel, n_pg=n_pg),
        grid_spec=grid_spec,
        out_shape=jax.ShapeDtypeStruct((n, 1, D_MODEL), F32),
        compiler_params=_params("parallel", "arbitrary"),
        name="sb_decode_attn",
    )(page_table, q_mat, bias_row, expand, *([cache_k] * n_pg), *([cache_v] * n_pg))


def _row(v):
    return v.reshape(1, -1).astype(F32)


def _pad_lanes(v, offset):
    out = jnp.zeros((1, LANES), F32)
    return lax.dynamic_update_slice(out, v.reshape(1, -1).astype(F32), (0, offset))


def kernel(x_prompt, x_sample, state_gdn_conv, state_gdn_S, cache_sb_k, cache_sb_v, state_conv, page_table,
           meta_tokens, ln_g, ln_b, ffn_w_gate, ffn_w_up, ffn_w_down,
           gdn_w_in, gdn_conv_w, gdn_a_log, gdn_dt_bias, gdn_norm_w, gdn_w_out,
           sb_w_qkv, sb_w_out, sb_logit_bias,
           cv_w_pw1, cv_b_pw1, cv_dw_w, cv_dw_b, cv_ln_g, cv_ln_b, cv_w_pw2, cv_b_pw2):
    nb, seq, _ = x_prompt.shape
    ns = x_sample.shape[0]
    assert x_sample.shape[1] == 1, "the sample group holds one new token per sequence"
    t_real = N_META + seq
    t_pad = -(-t_real // SEQ_ALIGN) * SEQ_ALIGN
    rows_p = nb * t_pad

    meta = jnp.broadcast_to(meta_tokens.astype(F32)[None], (nb, N_META, D_MODEL))
    hp = jnp.concatenate([meta, x_prompt, jnp.zeros((nb, t_pad - t_real, D_MODEL), F32)], axis=1)
    hp = hp.reshape(rows_p, D_MODEL)
    hs = x_sample.reshape(ns, D_MODEL)

    gdn_s_p, gdn_s_s, gdn_c_p, gdn_c_s = [], [], [], []
    sb_k_p, sb_v_p, sb_k_s, sb_v_s = [], [], [], []
    cv_p, cv_s = [], []
    zero_bias = jnp.zeros((1, D_MODEL), F32)

    def ffn(h, i, half, ln_idx):
        return ffn_ln(h, ffn_w_gate[i, half].astype(BF16), ffn_w_up[i, half].astype(BF16),
                      ffn_w_down[i, half].astype(BF16), _row(ln_g[i, ln_idx]), _row(ln_b[i, ln_idx]))

    for i in range(DEPTH):
        j, kind = i // N_MIXERS, i % N_MIXERS
        hp = ffn(hp, i, 0, 0)
        hs = ffn(hs, i, 0, 0)
        g1, b1 = _row(ln_g[i, 1]), _row(ln_b[i, 1])

        if kind == 0:
            w_in = gdn_w_in[j]
            w_qkv = w_in[:, :GDN_QKV].astype(BF16)
            w_z = w_in[:, GDN_QKV:GDN_QKV + GDN_Z].astype(BF16)
            w_ba = jnp.pad(w_in[:, GDN_QKV + GDN_Z:], ((0, 0), (0, LANES - 2 * GDN_HEADS))).astype(BF16)
            conv_w = gdn_conv_w[j].astype(F32)
            a_log_row = _pad_lanes(gdn_a_log[j], GDN_HEADS)
            dt_bias_row = _pad_lanes(gdn_dt_bias[j], GDN_HEADS)
            norm_w = _row(gdn_norm_w[j])
            w_out = gdn_w_out[j].astype(BF16)

            qkv_pre, z, ba = proj(hp, [w_qkv, w_z, w_ba])
            qkv_pre3 = qkv_pre.reshape(nb, t_pad, GDN_QKV)
            qkv_act, bg = gdn_act(qkv_pre3, ba.reshape(nb, t_pad, LANES), conv_w, a_log_row, dt_bias_row, t_real)
            o, s_fin = gdn_chunks(qkv_act, bg)
            hp = gdn_out_ln(o.reshape(rows_p, GDN_Z), z, hp, norm_w, w_out, g1, b1)
            gdn_s_p.append(s_fin)
            gdn_c_p.append(qkv_pre3[:, t_real - (GDN_CONV - 1):t_real])

            qkv_pre, z, ba = proj(hs, [w_qkv, w_z, w_ba])
            seq_s = jnp.concatenate([state_gdn_conv[j].astype(F32), qkv_pre[:, None]], axis=1)
            qkv_act, bg = gdn_act_sample(jnp.swapaxes(seq_s, 0, 1), ba, conv_w, a_log_row, dt_bias_row)
            hk = GDN_HEADS * GDN_DK
            q_col = qkv_act[:, :hk].reshape(ns, GDN_HEADS, GDN_DK, 1)
            k_col = qkv_act[:, hk:2 * hk].reshape(ns, GDN_HEADS, GDN_DK, 1)
            v_row = qkv_act[:, 2 * hk:].reshape(ns, 1, GDN_Z)
            o, s_fin = gdn_step(state_gdn_S[j].astype(F32), q_col, k_col, v_row, bg.reshape(ns, 1, LANES))
            hs = gdn_out_ln(o.reshape(ns, GDN_Z), z, hs, norm_w, w_out, g1, b1)
            gdn_s_s.append(s_fin)
            gdn_c_s.append(seq_s[:, 1:])

        elif kind == 1:
            w_qkv = sb_w_qkv[j].astype(BF16)
            w_out = sb_w_out[j].astype(BF16)
            bias = sb_logit_bias[j].astype(F32)

            (qkv,) = proj(hp, [w_qkv])
            qkv3 = qkv.reshape(nb, t_pad, 3 * D_MODEL)
            o = sb_prompt_attn(qkv3, bias)
            hp = proj_res_ln(o.reshape(rows_p, D_MODEL), hp, w_out, zero_bias, g1, b1)
            sb_k_p.append(qkv3[:, :t_real, D_MODEL:2 * D_MODEL].reshape(nb, t_real, SB_HEADS, SB_DH))
            sb_v_p.append(qkv3[:, :t_real, 2 * D_MODEL:].reshape(nb, t_real, SB_HEADS, SB_DH))

            (qkv,) = proj(hs, [w_qkv])
            q_t = jnp.swapaxes(qkv[:, :D_MODEL].reshape(ns, SB_HEADS, SB_DH), 1, 2)
            q_mat = jnp.pad(q_t, ((0, 0), (0, 0), (0, LANES - SB_HEADS))).astype(BF16)
            head_of_lane = jnp.arange(D_MODEL) // SB_DH
            expand = (jnp.arange(LANES)[:, None] == head_of_lane[None, :]).astype(BF16)
            bias_row = _pad_lanes(bias, 0)
            n_phys = cache_sb_k.shape[1]
            o = sb_decode_attn(q_mat, bias_row, expand,
                               cache_sb_k.reshape(-1, PAGE_SIZE * SB_HEADS, SB_DH).astype(F32),
                               cache_sb_v.reshape(-1, PAGE_SIZE * SB_HEADS, SB_DH).astype(F32),
                               page_table.astype(jnp.int32) + j * n_phys)
            hs = proj_res_ln(o.reshape(ns, D_MODEL), hs, w_out, zero_bias, g1, b1)
            sb_k_s.append(qkv[:, D_MODEL:2 * D_MODEL].reshape(ns, 1, SB_HEADS, SB_DH))
            sb_v_s.append(qkv[:, 2 * D_MODEL:].reshape(ns, 1, SB_HEADS, SB_DH))

        else:
            w1 = cv_w_pw1[j].astype(BF16)
            b_pw1 = _row(cv_b_pw1[j])
            dw_w = cv_dw_w[j].astype(F32)
            dw_b, cg, cb = _row(cv_dw_b[j]), _row(cv_ln_g[j]), _row(cv_ln_b[j])
            w2 = cv_w_pw2[j].astype(BF16)
            b_pw2 = _row(cv_b_pw2[j])

            u = glu_proj(hp, w1, b_pw1).reshape(nb, t_pad, D_MODEL)
            d = cv_conv(u, dw_w, dw_b, cg, cb)
            hp = proj_res_ln(d.reshape(rows_p, D_MODEL), hp, w2, b_pw2, g1, b1)
            cv_p.append(u[:, t_real - (CONV_WIDTH - 1):t_real])

            u = glu_proj(hs, w1, b_pw1)
            seq_s = jnp.concatenate([state_conv[j].astype(F32), u[:, None]], axis=1)
            d = cv_conv_sample(jnp.swapaxes(seq_s, 0, 1), dw_w, dw_b, cg, cb)
            hs = proj_res_ln(d, hs, w2, b_pw2, g1, b1)
            cv_s.append(seq_s[:, 1:])

        hp = ffn(hp, i, 1, 2)
        hs = ffn(hs, i, 1, 2)

    y_prompt = hp.reshape(nb, t_pad, D_MODEL)[:, N_META:t_real]
    y_sample = hs.reshape(ns, 1, D_MODEL)
    return (y_prompt, y_sample,
            jnp.stack(gdn_s_p), jnp.stack(gdn_s_s), jnp.stack(gdn_c_p), jnp.stack(gdn_c_s),
            jnp.stack(sb_k_p), jnp.stack(sb_v_p), jnp.stack(sb_k_s), jnp.stack(sb_v_s),
            jnp.stack(cv_p), jnp.stack(cv_s))
```

```python
import functools
import math

import jax
import jax.numpy as jnp
from jax import lax
from jax.experimental import pallas as pl
from jax.experimental.pallas import tpu as pltpu

F32 = jnp.float32
BF16 = jnp.bfloat16

D_MODEL = 1024
DEPTH = 4
N_MIXERS = 3
N_META = 16
ALPHA = (2 * DEPTH) ** 0.25
D_FF = 2816
LN_EPS = 1e-5
RMS_EPS = 1e-6
PAGE_SIZE = 128

GDN_HEADS = 8
GDN_DK = 128
GDN_DV = 128
GDN_CONV = 4
GDN_QKV = GDN_HEADS * (2 * GDN_DK + GDN_DV)
GDN_Z = GDN_HEADS * GDN_DV
GDN_CHUNK = 64

SB_HEADS = 8
SB_DH = 128
SB_TQ = 384
SB_TK = 128

CONV_WIDTH = 31
CONV_HALO = 32

SEQ_ALIGN = 384
ROW_TILE = 512
SEQ_TILE = 384
LANES = 128
SUBLANES = 8
VMEM_LIMIT = 56 * 1024 * 1024


def _sigmoid(x):
    return 1.0 / (1.0 + jnp.exp(-x))


def _silu(x):
    return x * _sigmoid(x)


def _softplus(x):
    return jnp.maximum(x, 0.0) + jnp.log1p(jnp.exp(-jnp.abs(x)))


def _layer_norm(y, g, b):
    mu = jnp.mean(y, axis=-1, keepdims=True)
    yc = y - mu
    var = jnp.mean(yc * yc, axis=-1, keepdims=True)
    return yc * lax.rsqrt(var + LN_EPS) * g + b


def _dot(a, b):
    return jnp.dot(a.astype(BF16), b.astype(BF16), preferred_element_type=F32)


def _dot_nt(a, b):
    return lax.dot_general(a.astype(BF16), b.astype(BF16), (((1,), (1,)), ((), ())), preferred_element_type=F32)


def _dot_tn(a, b):
    return lax.dot_general(a.astype(BF16), b.astype(BF16), (((0,), (0,)), ((), ())), preferred_element_type=F32)


def _split(a):
    hi = a.astype(BF16)
    lo = (a - hi.astype(F32)).astype(BF16)
    return hi, lo


def _dot3(a, b):
    a_hi, a_lo = _split(a)
    b_hi, b_lo = _split(b)
    lhs = jnp.concatenate([a_hi, a_lo, a_hi], axis=1)
    rhs = jnp.concatenate([b_hi, b_hi, b_lo], axis=0)
    return jnp.dot(lhs, rhs, preferred_element_type=F32)


def _dot2_exact_rhs(a, b_bf16):
    a_hi, a_lo = _split(a)
    d = functools.partial(jnp.dot, preferred_element_type=F32)
    return d(a_hi, b_bf16) + d(a_lo, b_bf16)


def _iota(shape, dim):
    return lax.broadcasted_iota(jnp.int32, shape, dim)


def _params(*sem):
    return pltpu.CompilerParams(dimension_semantics=sem, vmem_limit_bytes=VMEM_LIMIT)


def _const_spec(shape):
    nd = len(shape)
    return pl.BlockSpec(shape, lambda *_: (0,) * nd, pipeline_mode=pl.Buffered(1))


def _row_spec(tm, width):
    return pl.BlockSpec((tm, width), lambda i: (i, 0))


def _row_tile(rows):
    return ROW_TILE if rows % ROW_TILE == 0 else rows


FFN_SPLIT = 2


def _ffn_ln_kernel(x_ref, wg_ref, wu_ref, wd_ref, g_ref, b_ref, o_ref):
    x = x_ref[...]
    xb = x.astype(BF16)
    fc = D_FF // FFN_SPLIT
    acc = None
    for c in range(FFN_SPLIT):
        sl = slice(c * fc, (c + 1) * fc)
        gate = jnp.dot(xb, wg_ref[:, sl], preferred_element_type=F32)
        up = jnp.dot(xb, wu_ref[:, sl], preferred_element_type=F32)
        act = (_silu(gate) * up).astype(BF16)
        part = jnp.dot(act, wd_ref[sl, :], preferred_element_type=F32)
        acc = part if acc is None else acc + part
    o_ref[...] = _layer_norm(ALPHA * x + 0.5 * acc, g_ref[...], b_ref[...])


def ffn_ln(x, wg, wu, wd, g, b):
    rows = x.shape[0]
    tm = _row_tile(rows)
    return pl.pallas_call(
        _ffn_ln_kernel,
        grid=(rows // tm,),
        in_specs=[_row_spec(tm, D_MODEL), _const_spec(wg.shape), _const_spec(wu.shape), _const_spec(wd.shape),
                  _const_spec((1, D_MODEL)), _const_spec((1, D_MODEL))],
        out_specs=_row_spec(tm, D_MODEL),
        out_shape=jax.ShapeDtypeStruct((rows, D_MODEL), F32),
        compiler_params=_params("parallel"),
        name="ffn_ln",
    )(x, wg, wu, wd, g, b)


def _proj_kernel(x_ref, *refs, widths):
    n = len(widths)
    w_refs, o_refs = refs[:n], refs[n:]
    xb = x_ref[...].astype(BF16)
    for w_ref, o_ref, width in zip(w_refs, o_refs, widths):
        for c0 in range(0, width, D_MODEL):
            c1 = min(c0 + D_MODEL, width)
            o_ref[:, c0:c1] = jnp.dot(xb, w_ref[:, c0:c1], preferred_element_type=F32)


def proj(x, weights):
    rows = x.shape[0]
    tm = _row_tile(rows)
    widths = tuple(w.shape[1] for w in weights)
    outs = pl.pallas_call(
        functools.partial(_proj_kernel, widths=widths),
        grid=(rows // tm,),
        in_specs=[_row_spec(tm, D_MODEL)] + [_const_spec(w.shape) for w in weights],
        out_specs=[_row_spec(tm, n) for n in widths],
        out_shape=[jax.ShapeDtypeStruct((rows, n), F32) for n in widths],
        compiler_params=_params("parallel"),
        name="proj",
    )(x, *weights)
    return outs


def _proj_res_ln_kernel(a_ref, h_ref, w_ref, bias_ref, g_ref, b_ref, o_ref):
    y = jnp.dot(a_ref[...].astype(BF16), w_ref[...], preferred_element_type=F32) + bias_ref[...]
    o_ref[...] = _layer_norm(ALPHA * h_ref[...] + y, g_ref[...], b_ref[...])


def proj_res_ln(a, h, w, bias, g, b):
    rows = a.shape[0]
    tm = _row_tile(rows)
    return pl.pallas_call(
        _proj_res_ln_kernel,
        grid=(rows // tm,),
        in_specs=[_row_spec(tm, a.shape[1]), _row_spec(tm, D_MODEL), _const_spec(w.shape),
                  _const_spec((1, D_MODEL)), _const_spec((1, D_MODEL)), _const_spec((1, D_MODEL))],
        out_specs=_row_spec(tm, D_MODEL),
        out_shape=jax.ShapeDtypeStruct((rows, D_MODEL), F32),
        compiler_params=_params("parallel"),
        name="proj_res_ln",
    )(a, h, w, bias, g, b)


def _gdn_out_ln_kernel(o_ref, z_ref, h_ref, nw_ref, w_ref, g_ref, b_ref, out_ref):
    z = z_ref[...]
    gate = _silu(z)
    parts = []
    for hd in range(GDN_HEADS):
        sl = slice(hd * GDN_DV, (hd + 1) * GDN_DV)
        o = o_ref[:, sl]
        ms = jnp.mean(o * o, axis=-1, keepdims=True)
        parts.append((o * lax.rsqrt(ms + RMS_EPS) * nw_ref[...] * gate[:, sl]).astype(BF16))
    a = jnp.concatenate(parts, axis=-1)
    y = jnp.dot(a, w_ref[...], preferred_element_type=F32)
    out_ref[...] = _layer_norm(ALPHA * h_ref[...] + y, g_ref[...], b_ref[...])


def gdn_out_ln(o, z, h, norm_w, w_out, g, b):
    rows = o.shape[0]
    tm = _row_tile(rows)
    return pl.pallas_call(
        _gdn_out_ln_kernel,
        grid=(rows // tm,),
        in_specs=[_row_spec(tm, GDN_Z), _row_spec(tm, GDN_Z), _row_spec(tm, D_MODEL), _const_spec((1, GDN_DV)),
                  _const_spec(w_out.shape), _const_spec((1, D_MODEL)), _const_spec((1, D_MODEL))],
        out_specs=_row_spec(tm, D_MODEL),
        out_shape=jax.ShapeDtypeStruct((rows, D_MODEL), F32),
        compiler_params=_params("parallel"),
        name="gdn_out_ln",
    )(o, z, h, norm_w, w_out, g, b)


def _glu_kernel(x_ref, w_ref, bias_ref, o_ref):
    xb = x_ref[...].astype(BF16)
    a = jnp.dot(xb, w_ref[:, :D_MODEL], preferred_element_type=F32) + bias_ref[:, :D_MODEL]
    gate = jnp.dot(xb, w_ref[:, D_MODEL:], preferred_element_type=F32) + bias_ref[:, D_MODEL:]
    o_ref[...] = a * _sigmoid(gate)


def glu_proj(x, w, bias):
    rows = x.shape[0]
    tm = _row_tile(rows)
    return pl.pallas_call(
        _glu_kernel,
        grid=(rows // tm,),
        in_specs=[_row_spec(tm, D_MODEL), _const_spec(w.shape), _const_spec((1, 2 * D_MODEL))],
        out_specs=_row_spec(tm, D_MODEL),
        out_shape=jax.ShapeDtypeStruct((rows, D_MODEL), F32),
        compiler_params=_params("parallel"),
        name="glu_proj",
    )(x, w, bias)


def _halo_spec(ts, halo, width):
    per = ts // halo
    return pl.BlockSpec((None, halo, width), lambda b, t: (b, jnp.maximum(t * per - 1, 0), 0))


def _causal_taps(xx_ref, halo, rows, taps, w_ref, col):
    acc = None
    for k in range(taps):
        term = xx_ref[pl.ds(halo - k, rows), col] * w_ref[taps - 1 - k:taps - k, col]
        acc = term if acc is None else acc + term
    return acc


def _gdn_act_kernel(x_ref, halo_ref, ba_ref, cw_ref, alog_ref, dtb_ref, qkv_ref, bg_ref, xx_ref, *, ts, t_real):
    t = pl.program_id(1)
    halo = halo_ref[...]
    xx_ref[0:SUBLANES, :] = jnp.where(t == 0, 0.0, halo)
    xx_ref[SUBLANES:, :] = x_ref[...]
    hk = GDN_HEADS * GDN_DK
    for c in range(GDN_QKV // LANES):
        col = slice(c * LANES, (c + 1) * LANES)
        y = _silu(_causal_taps(xx_ref, SUBLANES, ts, GDN_CONV, cw_ref, col))
        if c * LANES < 2 * hk:
            y = y * lax.rsqrt(jnp.sum(y * y, axis=-1, keepdims=True) + RMS_EPS)
            if c * LANES < hk:
                y = y * GDN_DK ** -0.5
        qkv_ref[:, col] = y
    ba = ba_ref[...]
    row = t * ts + _iota(ba.shape, 0)
    lane = _iota(ba.shape, 1)
    beta = _sigmoid(ba)
    g = -jnp.exp(alog_ref[...]) * _softplus(ba + dtb_ref[...])
    bg_ref[...] = jnp.where(row < t_real, jnp.where(lane < GDN_HEADS, beta, g), 0.0)


def gdn_act(qkv_pre, ba, conv_w, a_log_row, dt_bias_row, t_real):
    nb, t_pad, _ = qkv_pre.shape
    ts = SEQ_TILE
    kern = functools.partial(_gdn_act_kernel, ts=ts, t_real=t_real)
    seq_spec = lambda width: pl.BlockSpec((None, ts, width), lambda b, t: (b, t, 0))
    return pl.pallas_call(
        kern,
        grid=(nb, t_pad // ts),
        in_specs=[seq_spec(GDN_QKV), _halo_spec(ts, SUBLANES, GDN_QKV), seq_spec(LANES),
                  _const_spec(conv_w.shape), _const_spec((1, LANES)), _const_spec((1, LANES))],
        out_specs=[seq_spec(GDN_QKV), seq_spec(LANES)],
        out_shape=[jax.ShapeDtypeStruct((nb, t_pad, GDN_QKV), F32), jax.ShapeDtypeStruct((nb, t_pad, LANES), F32)],
        scratch_shapes=[pltpu.VMEM((ts + SUBLANES, GDN_QKV), F32)],
        compiler_params=_params("parallel", "parallel"),
        name="gdn_act",
    )(qkv_pre, qkv_pre, ba, conv_w, a_log_row, dt_bias_row)


def _cv_conv_kernel(u_ref, halo_ref, w_ref, wb_ref, g_ref, b_ref, o_ref, xx_ref, ys_ref, *, ts):
    t = pl.program_id(1)
    xx_ref[0:CONV_HALO, :] = jnp.where(t == 0, 0.0, halo_ref[...])
    xx_ref[CONV_HALO:, :] = u_ref[...]
    n_keep = ts + CONV_HALO - SUBLANES
    for b in range(1, SUBLANES):
        ys_ref[b - 1, pl.ds(SUBLANES, n_keep), :] = xx_ref[pl.ds(SUBLANES - b, n_keep), :]
    acc = None
    for k in range(CONV_WIDTH):
        a, b = divmod(k, SUBLANES)
        start = CONV_HALO - SUBLANES * a
        x_k = xx_ref[pl.ds(start, ts), :] if b == 0 else ys_ref[b - 1, pl.ds(start, ts), :]
        term = x_k * w_ref[CONV_WIDTH - 1 - k:CONV_WIDTH - k, :]
        acc = term if acc is None else acc + term
    o_ref[...] = _silu(_layer_norm(acc + wb_ref[...], g_ref[...], b_ref[...]))


def cv_conv(u, dw_w, dw_b, ln_g, ln_b):
    nb, t_pad, _ = u.shape
    ts = SEQ_TILE
    seq_spec = pl.BlockSpec((None, ts, D_MODEL), lambda b, t: (b, t, 0))
    return pl.pallas_call(
        functools.partial(_cv_conv_kernel, ts=ts),
        grid=(nb, t_pad // ts),
        in_specs=[seq_spec, _halo_spec(ts, CONV_HALO, D_MODEL), _const_spec(dw_w.shape),
                  _const_spec((1, D_MODEL)), _const_spec((1, D_MODEL)), _const_spec((1, D_MODEL))],
        out_specs=seq_spec,
        out_shape=jax.ShapeDtypeStruct((nb, t_pad, D_MODEL), F32),
        scratch_shapes=[pltpu.VMEM((ts + CONV_HALO, D_MODEL), F32),
                        pltpu.VMEM((SUBLANES - 1, ts + CONV_HALO, D_MODEL), F32)],
        compiler_params=_params("parallel", "parallel"),
        name="cv_conv",
    )(u, u, dw_w, dw_b, ln_g, ln_b)


def _gdn_act_sample_kernel(seq_ref, ba_ref, cw_ref, alog_ref, dtb_ref, qkv_ref, bg_ref):
    hk = GDN_HEADS * GDN_DK
    for c in range(GDN_QKV // LANES):
        col = slice(c * LANES, (c + 1) * LANES)
        acc = None
        for i in range(GDN_CONV):
            term = seq_ref[i, :, col] * cw_ref[i:i + 1, col]
            acc = term if acc is None else acc + term
        y = _silu(acc)
        if c * LANES < 2 * hk:
            y = y * lax.rsqrt(jnp.sum(y * y, axis=-1, keepdims=True) + RMS_EPS)
            if c * LANES < hk:
                y = y * GDN_DK ** -0.5
        qkv_ref[:, col] = y
    ba = ba_ref[...]
    lane = _iota(ba.shape, 1)
    g = -jnp.exp(alog_ref[...]) * _softplus(ba + dtb_ref[...])
    bg_ref[...] = jnp.where(lane < GDN_HEADS, _sigmoid(ba), g)


def gdn_act_sample(seq, ba, conv_w, a_log_row, dt_bias_row):
    n = seq.shape[1]
    return pl.pallas_call(
        _gdn_act_sample_kernel,
        out_shape=[jax.ShapeDtypeStruct((n, GDN_QKV), F32), jax.ShapeDtypeStruct((n, LANES), F32)],
        compiler_params=pltpu.CompilerParams(vmem_limit_bytes=VMEM_LIMIT),
        name="gdn_act_sample",
    )(seq, ba, conv_w, a_log_row, dt_bias_row)


def _cv_conv_sample_kernel(seq_ref, w_ref, wb_ref, g_ref, b_ref, o_ref):
    acc = None
    for i in range(CONV_WIDTH):
        term = seq_ref[i] * w_ref[i:i + 1, :]
        acc = term if acc is None else acc + term
    o_ref[...] = _silu(_layer_norm(acc + wb_ref[...], g_ref[...], b_ref[...]))


def cv_conv_sample(seq, dw_w, dw_b, ln_g, ln_b):
    n = seq.shape[1]
    return pl.pallas_call(
        _cv_conv_sample_kernel,
        out_shape=jax.ShapeDtypeStruct((n, D_MODEL), F32),
        compiler_params=pltpu.CompilerParams(vmem_limit_bytes=VMEM_LIMIT),
        name="cv_conv_sample",
    )(seq, dw_w, dw_b, ln_g, ln_b)


GDN_CHUNKS_PER_STEP = 2


def _gdn_chunk_kernel(qkv_ref, bg_ref, o_ref, s_out_ref, s_ref):
    c = pl.program_id(1)
    n_c = pl.num_programs(1)
    cs = GDN_CHUNK
    n_sub = GDN_CHUNKS_PER_STEP
    hk = GDN_HEADS * GDN_DK

    @pl.when(c == 0)
    def _():
        s_ref[...] = jnp.zeros_like(s_ref)

    row = _iota((cs, LANES), 0)
    ri = _iota((cs, cs), 0)
    ci = _iota((cs, cs), 1)
    lower = ri >= ci
    strict = ri > ci
    eye = (ri == ci).astype(F32)

    gcs, gc_ts = [], []
    for j in range(n_sub):
        gc = bg_ref[j * cs:(j + 1) * cs, :]
        shift = 1
        while shift < cs:
            gc = gc + jnp.where(row >= shift, pltpu.roll(gc, shift, axis=0), 0.0)
            shift *= 2
        gcs.append(gc)
        gc_ts.append(gc.T)

    items = [(j, hd) for j in range(n_sub) for hd in range(GDN_HEADS)]
    idx = range(len(items))
    rows = [slice(j * cs, (j + 1) * cs) for j, _ in items]
    q = [qkv_ref[rows[i], hd * GDN_DK:(hd + 1) * GDN_DK] for i, (j, hd) in enumerate(items)]
    k = [qkv_ref[rows[i], hk + hd * GDN_DK:hk + (hd + 1) * GDN_DK] for i, (j, hd) in enumerate(items)]
    v = [qkv_ref[rows[i], 2 * hk + hd * GDN_DV:2 * hk + (hd + 1) * GDN_DV] for i, (j, hd) in enumerate(items)]
    beta = [bg_ref[rows[i], hd:hd + 1] for i, (j, hd) in enumerate(items)]
    g_col = [gcs[j][:, GDN_HEADS + hd:GDN_HEADS + hd + 1] for j, hd in items]
    g_row = [gc_ts[j][GDN_HEADS + hd:GDN_HEADS + hd + 1, :] for j, hd in items]
    g_last = [gcs[j][cs - 1:cs, GDN_HEADS + hd:GDN_HEADS + hd + 1] for j, hd in items]
    decay = [jnp.exp(jnp.where(lower, g_col[i] - g_row[i], -jnp.inf)) for i in idx]
    eg = [jnp.exp(g_col[i]) for i in idx]
    kb = [k[i] * beta[i] for i in idx]

    kq = [_dot_nt(jnp.concatenate([kb[i], q[i]], axis=0), k[i]) for i in idx]
    qkd = [kq[i][cs:] * decay[i] for i in idx]
    p = [-jnp.where(strict, kq[i][:cs] * decay[i], 0.0) for i in idx]
    t_inv = [eye + p[i] for i in idx]
    p = [_dot3(p[i], p[i]) for i in idx]
    span = 2
    while span < cs:
        m = [_dot3(p[i], jnp.concatenate([t_inv[i], p[i]], axis=1)) for i in idx]
        t_inv = [t_inv[i] + m[i][:, :cs] for i in idx]
        p = [m[i][:, cs:] for i in idx]
        span *= 2
    sol = [_dot3(t_inv[i], jnp.concatenate([v[i] * beta[i], kb[i] * eg[i]], axis=1)) for i in idx]
    qe = [q[i] * eg[i] for i in idx]
    kd = [k[i] * jnp.exp(g_last[i] - g_col[i]) for i in idx]
    e_last = [jnp.exp(g_last[i]) for i in idx]

    s = [s_ref[hd] for hd in range(GDN_HEADS)]
    for j in range(n_sub):
        sub = [j * GDN_HEADS + hd for hd in range(GDN_HEADS)]
        wq = [_dot(jnp.concatenate([sol[i][:, GDN_DV:], qe[i]], axis=0), s[hd]) for hd, i in enumerate(sub)]
        v_new = [sol[i][:, :GDN_DV] - wq[hd][:cs] for hd, i in enumerate(sub)]
        o_in = [_dot(qkd[i], v_new[hd]) for hd, i in enumerate(sub)]
        s_add = [_dot_tn(kd[i], v_new[hd]) for hd, i in enumerate(sub)]
        for hd, i in enumerate(sub):
            o_ref[rows[i], hd * GDN_DV:(hd + 1) * GDN_DV] = wq[hd][cs:] + o_in[hd]
        s = [s[hd] * e_last[i] + s_add[hd] for hd, i in enumerate(sub)]
    for hd in range(GDN_HEADS):
        s_ref[hd] = s[hd]

    @pl.when(c == n_c - 1)
    def _():
        s_out_ref[...] = s_ref[...]


def gdn_chunks(qkv, bg):
    nb, t_pad, _ = qkv.shape
    rows = GDN_CHUNK * GDN_CHUNKS_PER_STEP
    return pl.pallas_call(
        _gdn_chunk_kernel,
        grid=(nb, t_pad // rows),
        in_specs=[pl.BlockSpec((None, rows, GDN_QKV), lambda b, c: (b, c, 0)),
                  pl.BlockSpec((None, rows, LANES), lambda b, c: (b, c, 0))],
        out_specs=[pl.BlockSpec((None, rows, GDN_Z), lambda b, c: (b, c, 0)),
                   pl.BlockSpec((None, GDN_HEADS, GDN_DK, GDN_DV), lambda b, c: (b, 0, 0, 0))],
        out_shape=[jax.ShapeDtypeStruct((nb, t_pad, GDN_Z), F32),
                   jax.ShapeDtypeStruct((nb, GDN_HEADS, GDN_DK, GDN_DV), F32)],
        scratch_shapes=[pltpu.VMEM((GDN_HEADS, GDN_DK, GDN_DV), F32)],
        compiler_params=_params("parallel", "arbitrary"),
        name="gdn_chunks",
    )(qkv, bg)


GDN_STEP_GROUP = 4


def _gdn_step_kernel(s_ref, qk_ref, v_ref, bg_ref, o_ref, s_out_ref, *, group):
    for i in range(group):
        qk_t = qk_ref[i].T
        for hd in range(GDN_HEADS):
            beta = bg_ref[i, :, hd:hd + 1]
            g = bg_ref[i, :, GDN_HEADS + hd:GDN_HEADS + hd + 1]
            q_col = qk_t[:, hd:hd + 1]
            k_col = qk_t[:, GDN_HEADS + hd:GDN_HEADS + hd + 1]
            v_row = v_ref[i, :, hd * GDN_DV:(hd + 1) * GDN_DV]
            s = s_ref[i, hd] * jnp.exp(g)
            ks = jnp.sum(s * k_col, axis=0, keepdims=True)
            delta = (v_row - ks) * beta
            s = s + k_col * delta
            s_out_ref[i, hd] = s
            o_ref[i, :, hd * GDN_DV:(hd + 1) * GDN_DV] = jnp.sum(s * q_col, axis=0, keepdims=True)


def gdn_step(s0, qk, v, bg):
    n = s0.shape[0]
    group = math.gcd(n, GDN_STEP_GROUP)
    s_spec = pl.BlockSpec((group, GDN_HEADS, GDN_DK, GDN_DV), lambda i: (i, 0, 0, 0))
    return pl.pallas_call(
        functools.partial(_gdn_step_kernel, group=group),
        grid=(n // group,),
        in_specs=[s_spec,
                  pl.BlockSpec((group, 2 * GDN_HEADS, GDN_DK), lambda i: (i, 0, 0)),
                  pl.BlockSpec((group, 1, GDN_Z), lambda i: (i, 0, 0)),
                  pl.BlockSpec((group, 1, LANES), lambda i: (i, 0, 0))],
        out_specs=[pl.BlockSpec((group, 1, GDN_Z), lambda i: (i, 0, 0)), s_spec],
        out_shape=[jax.ShapeDtypeStruct((n, 1, GDN_Z), F32),
                   jax.ShapeDtypeStruct((n, GDN_HEADS, GDN_DK, GDN_DV), F32)],
        compiler_params=_params("parallel"),
        name="gdn_step",
    )(s0, qk, v, bg)


SB_HEAD_GROUP = 2


def _sb_prompt_kernel(bias_ref, q_ref, k_ref, v_ref, o_ref):
    hg = pl.program_id(1)
    qi = pl.program_id(2)
    tq, tk, dh = SB_TQ, SB_TK, SB_DH
    n_sub = tq // tk
    heads = range(SB_HEAD_GROUP)
    bias = [bias_ref[hg * SB_HEAD_GROUP + h] for h in heads]
    q = [(q_ref[:, h * dh:(h + 1) * dh] * dh ** -0.5).astype(BF16) for h in heads]
    tr = _iota((2 * tk, 2 * tk), 0) & (tk - 1)
    tc = _iota((2 * tk, 2 * tk), 1)
    tail = jnp.where((tc >= tk) | (tr > tc), -1.0, 0.0).astype(BF16)
    below_diag = _iota((tq, tq), 1) < _iota((tq, tq), 0)

    def tile(k0, masked, acc, run):
        kblk = [k_ref[pl.ds(k0, tq), h * dh:(h + 1) * dh].astype(BF16) for h in heads]
        vblk = [v_ref[pl.ds(k0, tq), h * dh:(h + 1) * dh].astype(BF16) for h in heads]
        z = [lax.dot_general(q[h], kblk[h], (((1,), (1,)), ((), ())), preferred_element_type=F32) + bias[h]
             for h in heads]
        z_pos = [jnp.maximum(z[h], 0.0) for h in heads]
        z_neg = [jnp.minimum(z[h], 0.0) for h in heads]
        lg = [jnp.log(1.0 + jnp.exp(z_neg[h] - z_pos[h])) for h in heads]
        sp = [z_pos[h] + lg[h] for h in heads]
        log_beta = [z_neg[h] - lg[h] for h in heads]
        if masked:
            sp = [jnp.where(below_diag, sp[h], 0.0) for h in heads]
        split = [_split(sp[h]) for h in heads]
        res = [[jnp.dot(jnp.concatenate([part[:, s * tk:(s + 1) * tk] for part in split[h]], axis=1),
                        tail, preferred_element_type=F32) for s in range(n_sub)] for h in heads]
        acc, run = list(acc), list(run)
        for h in heads:
            w = [None] * n_sub
            for s in reversed(range(n_sub)):
                w[s] = jnp.exp(log_beta[h][:, s * tk:(s + 1) * tk] + (res[h][s][:, :tk] + run[h]))
                run[h] = run[h] + res[h][s][:, tk:]
            w = jnp.concatenate(w, axis=1)
            if masked:
                w = jnp.where(below_diag, w, 0.0)
            acc[h] = acc[h] + jnp.dot(w.astype(BF16), vblk[h], preferred_element_type=F32)
        return tuple(acc), tuple(run)

    zeros = lambda width: tuple(jnp.zeros((tq, width), F32) for _ in heads)
    acc, run = tile(pl.multiple_of(qi * tq, tq), True, zeros(dh), zeros(tk))

    def body(i, carry):
        return tile(pl.multiple_of((qi - 1 - i) * tq, tq), False, *carry)

    acc, _ = lax.fori_loop(0, qi, body, (acc, run))
    for h in heads:
        o_ref[:, h * dh:(h + 1) * dh] = acc[h]


def sb_prompt_attn(qkv, bias):
    nb, t_pad, _ = qkv.shape
    tq = SB_TQ
    gw = SB_HEAD_GROUP * SB_DH
    n_groups = SB_HEADS // SB_HEAD_GROUP
    grid_spec = pltpu.PrefetchScalarGridSpec(
        num_scalar_prefetch=0,
        grid=(nb, n_groups, t_pad // tq),
        in_specs=[pl.BlockSpec(memory_space=pltpu.SMEM),
                  pl.BlockSpec((None, tq, gw), lambda b, g, i: (b, i, g)),
                  pl.BlockSpec((None, t_pad, gw), lambda b, g, i: (b, 0, n_groups + g)),
                  pl.BlockSpec((None, t_pad, gw), lambda b, g, i: (b, 0, 2 * n_groups + g))],
        out_specs=pl.BlockSpec((None, tq, gw), lambda b, g, i: (b, i, g)),
    )
    return pl.pallas_call(
        _sb_prompt_kernel,
        grid_spec=grid_spec,
        out_shape=jax.ShapeDtypeStruct((nb, t_pad, D_MODEL), F32),
        compiler_params=_params("parallel", "parallel", "arbitrary"),
        name="sb_prompt_attn",
    )(bias, qkv, qkv, qkv)


SB_PAGES_PER_STEP = 8


def _sb_decode_kernel(pt_ref, q_ref, bias_ref, expand_ref, *refs, n_pg):
    k_refs, v_refs = refs[:n_pg], refs[n_pg:2 * n_pg]
    o_ref, acc_ref, run_ref = refs[2 * n_pg:]
    step = pl.program_id(1)
    n_steps = pl.num_programs(1)
    ps = PAGE_SIZE
    pages = range(n_pg)

    @pl.when(step == 0)
    def _():
        acc_ref[...] = jnp.zeros_like(acc_ref)
        run_ref[...] = jnp.zeros_like(run_ref)

    q_mat = q_ref[...]
    lane = _iota((ps, LANES), 1)
    z = []
    for i in pages:
        zi = jnp.zeros((ps, LANES), F32)
        for hd in range(SB_HEADS):
            k_head = k_refs[i][pl.ds(hd, ps, stride=SB_HEADS), :]
            zh = jnp.dot(k_head.astype(BF16), q_mat, preferred_element_type=F32)
            zi = jnp.where(lane == hd, zh, zi)
        z.append(zi * SB_DH ** -0.5 + bias_ref[...])
    lg = [jnp.log(1.0 + jnp.exp(-jnp.abs(z[i]))) for i in pages]
    sp = [jnp.maximum(z[i], 0.0) + lg[i] for i in pages]
    log_beta = [jnp.minimum(z[i], 0.0) - lg[i] for i in pages]
    nr = _iota((ps, 2 * ps), 0)
    nc = _iota((ps, 2 * ps), 1) & (ps - 1)
    newer = jnp.where(nc > nr, -1.0, 0.0).astype(BF16)
    after = [jnp.dot(newer, jnp.concatenate(_split(sp[i]), axis=0), preferred_element_type=F32) for i in pages]
    run = run_ref[...]
    w = []
    for i in pages:
        w.append(jnp.exp(log_beta[i] + after[i] + run).astype(BF16))
        run = run + (after[i][0:1, :] - sp[i][0:1, :])
    run_ref[...] = run
    w_wide = [jnp.dot(w[i], expand_ref[...], preferred_element_type=F32) for i in pages]
    for hd in range(SB_HEADS):
        part = None
        for i in pages:
            v_head = v_refs[i][pl.ds(hd, ps, stride=SB_HEADS), :]
            prod = w_wide[i][:, hd * SB_DH:(hd + 1) * SB_DH] * v_head
            prod = jnp.sum(prod.reshape(ps // SUBLANES, SUBLANES, SB_DH), axis=0)
            part = prod if part is None else part + prod
        acc_ref[hd] += part

    @pl.when(step == n_steps - 1)
    def _():
        for hd in range(SB_HEADS):
            o_ref[:, hd * SB_DH:(hd + 1) * SB_DH] = jnp.sum(acc_ref[hd], axis=0, keepdims=True)


def sb_decode_attn(q_mat, bias_row, expand, cache_k, cache_v, page_table):
    n, n_pages = page_table.shape
    n_pg = math.gcd(n_pages, SB_PAGES_PER_STEP)

    def page_spec(i):
        return pl.BlockSpec((None, PAGE_SIZE * SB_HEADS, SB_DH),
                            lambda s, p, pt: (pt[s, n_pages - 1 - (p * n_pg + i)], 0, 0))

    grid_spec = pltpu.PrefetchScalarGridSpec(
        num_scalar_prefetch=1,
        grid=(n, n_pages // n_pg),
        in_specs=[pl.BlockSpec((None, SB_DH, LANES), lambda s, p, pt: (s, 0, 0)),
                  pl.BlockSpec((1, LANES), lambda s, p, pt: (0, 0)),
                  pl.BlockSpec((LANES, D_MODEL), lambda s, p, pt: (0, 0))]
                 + [page_spec(i) for i in range(n_pg)] * 2,
        out_specs=pl.BlockSpec((None, 1, D_MODEL), lambda s, p, pt: (s, 0, 0)),
        scratch_shapes=[pltpu.VMEM((SB_HEADS, SUBLANES, SB_DH), F32), pltpu.VMEM((1, LANES), F32)],
    )
    return pl.pallas_call(
        functools.partial(_sb_decode_kernel, n_pg=n_pg),
        grid_spec=grid_spec,
        out_shape=jax.ShapeDtypeStruct((n, 1, D_MODEL), F32),
        compiler_params=_params("parallel", "arbitrary"),
        name="sb_decode_attn",
    )(page_table, q_mat, bias_row, expand, *([cache_k] * n_pg), *([cache_v] * n_pg))


def _row(v):
    return v.reshape(1, -1).astype(F32)


def _pad_lanes(v, offset):
    out = jnp.zeros((1, LANES), F32)
    return lax.dynamic_update_slice(out, v.reshape(1, -1).astype(F32), (0, offset))


def kernel(x_prompt, x_sample, state_gdn_conv, state_gdn_S, cache_sb_k, cache_sb_v, state_conv, page_table,
           meta_tokens, ln_g, ln_b, ffn_w_gate, ffn_w_up, ffn_w_down,
           gdn_w_in, gdn_conv_w, gdn_a_log, gdn_dt_bias, gdn_norm_w, gdn_w_out,
           sb_w_qkv, sb_w_out, sb_logit_bias,
           cv_w_pw1, cv_b_pw1, cv_dw_w, cv_dw_b, cv_ln_g, cv_ln_b, cv_w_pw2, cv_b_pw2):
    nb, seq, _ = x_prompt.shape
    ns = x_sample.shape[0]
    assert x_sample.shape[1] == 1, "the sample group holds one new token per sequence"
    t_real = N_META + seq
    t_pad = -(-t_real // SEQ_ALIGN) * SEQ_ALIGN
    rows_p = nb * t_pad

    meta = jnp.broadcast_to(meta_tokens.astype(F32)[None], (nb, N_META, D_MODEL))
    hp = jnp.concatenate([meta, x_prompt, jnp.zeros((nb, t_pad - t_real, D_MODEL), F32)], axis=1)
    hp = hp.reshape(rows_p, D_MODEL)
    hs = x_sample.reshape(ns, D_MODEL)

    gdn_s_p, gdn_s_s, gdn_c_p, gdn_c_s = [], [], [], []
    sb_k_p, sb_v_p, sb_k_s, sb_v_s = [], [], [], []
    cv_p, cv_s = [], []
    zero_bias = jnp.zeros((1, D_MODEL), F32)

    def ffn(h, i, half, ln_idx):
        return ffn_ln(h, ffn_w_gate[i, half].astype(BF16), ffn_w_up[i, half].astype(BF16),
                      ffn_w_down[i, half].astype(BF16), _row(ln_g[i, ln_idx]), _row(ln_b[i, ln_idx]))

    for i in range(DEPTH):
        j, kind = i // N_MIXERS, i % N_MIXERS
        hp = ffn(hp, i, 0, 0)
        hs = ffn(hs, i, 0, 0)
        g1, b1 = _row(ln_g[i, 1]), _row(ln_b[i, 1])

        if kind == 0:
            w_in = gdn_w_in[j]
            w_qkv = w_in[:, :GDN_QKV].astype(BF16)
            w_z = w_in[:, GDN_QKV:GDN_QKV + GDN_Z].astype(BF16)
            w_ba = jnp.pad(w_in[:, GDN_QKV + GDN_Z:], ((0, 0), (0, LANES - 2 * GDN_HEADS))).astype(BF16)
            conv_w = gdn_conv_w[j].astype(F32)
            a_log_row = _pad_lanes(gdn_a_log[j], GDN_HEADS)
            dt_bias_row = _pad_lanes(gdn_dt_bias[j], GDN_HEADS)
            norm_w = _row(gdn_norm_w[j])
            w_out = gdn_w_out[j].astype(BF16)

            qkv_pre, z, ba = proj(hp, [w_qkv, w_z, w_ba])
            qkv_pre3 = qkv_pre.reshape(nb, t_pad, GDN_QKV)
            qkv_act, bg = gdn_act(qkv_pre3, ba.reshape(nb, t_pad, LANES), conv_w, a_log_row, dt_bias_row, t_real)
            o, s_fin = gdn_chunks(qkv_act, bg)
            hp = gdn_out_ln(o.reshape(rows_p, GDN_Z), z, hp, norm_w, w_out, g1, b1)
            gdn_s_p.append(s_fin)
            gdn_c_p.append(qkv_pre3[:, t_real - (GDN_CONV - 1):t_real])

            qkv_pre, z, ba = proj(hs, [w_qkv, w_z, w_ba])
            seq_s = jnp.concatenate([state_gdn_conv[j].astype(F32), qkv_pre[:, None]], axis=1)
            qkv_act, bg = gdn_act_sample(jnp.swapaxes(seq_s, 0, 1), ba, conv_w, a_log_row, dt_bias_row)
            hk = GDN_HEADS * GDN_DK
            qk_rows = qkv_act[:, :2 * hk].reshape(ns, 2 * GDN_HEADS, GDN_DK)
            v_row = qkv_act[:, 2 * hk:].reshape(ns, 1, GDN_Z)
            o, s_fin = gdn_step(state_gdn_S[j].astype(F32), qk_rows, v_row, bg.reshape(ns, 1, LANES))
            hs = gdn_out_ln(o.reshape(ns, GDN_Z), z, hs, norm_w, w_out, g1, b1)
            gdn_s_s.append(s_fin)
            gdn_c_s.append(seq_s[:, 1:])

        elif kind == 1:
            w_qkv = sb_w_qkv[j].astype(BF16)
            w_out = sb_w_out[j].astype(BF16)
            bias = sb_logit_bias[j].astype(F32)

            (qkv,) = proj(hp, [w_qkv])
            qkv3 = qkv.reshape(nb, t_pad, 3 * D_MODEL)
            o = sb_prompt_attn(qkv3, bias)
            hp = proj_res_ln(o.reshape(rows_p, D_MODEL), hp, w_out, zero_bias, g1, b1)
            sb_k_p.append(qkv3[:, :t_real, D_MODEL:2 * D_MODEL].reshape(nb, t_real, SB_HEADS, SB_DH))
            sb_v_p.append(qkv3[:, :t_real, 2 * D_MODEL:].reshape(nb, t_real, SB_HEADS, SB_DH))

            (qkv,) = proj(hs, [w_qkv])
            q_t = jnp.swapaxes(qkv[:, :D_MODEL].reshape(ns, SB_HEADS, SB_DH), 1, 2)
            q_mat = jnp.pad(q_t, ((0, 0), (0, 0), (0, LANES - SB_HEADS))).astype(BF16)
            head_of_lane = jnp.arange(D_MODEL) // SB_DH
            expand = (jnp.arange(LANES)[:, None] == head_of_lane[None, :]).astype(BF16)
            bias_row = _pad_lanes(bias, 0)
            n_phys = cache_sb_k.shape[1]
            o = sb_decode_attn(q_mat, bias_row, expand,
                               cache_sb_k.reshape(-1, PAGE_SIZE * SB_HEADS, SB_DH).astype(F32),
                               cache_sb_v.reshape(-1, PAGE_SIZE * SB_HEADS, SB_DH).astype(F32),
                               page_table.astype(jnp.int32) + j * n_phys)
            hs = proj_res_ln(o.reshape(ns, D_MODEL), hs, w_out, zero_bias, g1, b1)
            sb_k_s.append(qkv[:, D_MODEL:2 * D_MODEL].reshape(ns, 1, SB_HEADS, SB_DH))
            sb_v_s.append(qkv[:, 2 * D_MODEL:].reshape(ns, 1, SB_HEADS, SB_DH))

        else:
            w1 = cv_w_pw1[j].astype(BF16)
            b_pw1 = _row(cv_b_pw1[j])
            dw_w = cv_dw_w[j].astype(F32)
            dw_b, cg, cb = _row(cv_dw_b[j]), _row(cv_ln_g[j]), _row(cv_ln_b[j])
            w2 = cv_w_pw2[j].astype(BF16)
            b_pw2 = _row(cv_b_pw2[j])

            u = glu_proj(hp, w1, b_pw1).reshape(nb, t_pad, D_MODEL)
            d = cv_conv(u, dw_w, dw_b, cg, cb)
            hp = proj_res_ln(d.reshape(rows_p, D_MODEL), hp, w2, b_pw2, g1, b1)
            cv_p.append(u[:, t_real - (CONV_WIDTH - 1):t_real])

            u = glu_proj(hs, w1, b_pw1)
            seq_s = jnp.concatenate([state_conv[j].astype(F32), u[:, None]], axis=1)
            d = cv_conv_sample(jnp.swapaxes(seq_s, 0, 1), dw_w, dw_b, cg, cb)
            hs = proj_res_ln(d, hs, w2, b_pw2, g1, b1)
            cv_s.append(seq_s[:, 1:])

        hp = ffn(hp, i, 1, 2)
        hs = ffn(hs, i, 1, 2)

    y_prompt = hp.reshape(nb, t_pad, D_MODEL)[:, N_META:t_real]
    y_sample = hs.reshape(ns, 1, D_MODEL)
    return (y_prompt, y_sample,
            jnp.stack(gdn_s_p), jnp.stack(gdn_s_s), jnp.stack(gdn_c_p), jnp.stack(gdn_c_s),
            jnp.stack(sb_k_p), jnp.stack(sb_v_p), jnp.stack(sb_k_s), jnp.stack(sb_v_s),
            jnp.stack(cv_p), jnp.stack(cv_s))
```

```python
import functools
import math

import jax
import jax.numpy as jnp
from jax import lax
from jax.experimental import pallas as pl
from jax.experimental.pallas import tpu as pltpu

F32 = jnp.float32
BF16 = jnp.bfloat16

D_MODEL = 1024
DEPTH = 4
N_MIXERS = 3
N_META = 16
ALPHA = (2 * DEPTH) ** 0.25
D_FF = 2816
LN_EPS = 1e-5
RMS_EPS = 1e-6
PAGE_SIZE = 128
LOG2_E = math.log2(math.e)

GDN_HEADS = 8
GDN_DK = 128
GDN_DV = 128
GDN_CONV = 4
GDN_QKV = GDN_HEADS * (2 * GDN_DK + GDN_DV)
GDN_Z = GDN_HEADS * GDN_DV
GDN_CHUNK = 64

SB_HEADS = 8
SB_DH = 128
SB_TQ = 384
SB_TK = 128

CONV_WIDTH = 31
CONV_HALO = 32

SEQ_ALIGN = 384
ROW_TILE = 512
SEQ_TILE = 384
LANES = 128
SUBLANES = 8
VMEM_LIMIT = 56 * 1024 * 1024


def _sigmoid(x):
    return 1.0 / (1.0 + jnp.exp(-x))


def _silu(x):
    return x * _sigmoid(x)


def _softplus(x):
    return jnp.maximum(x, 0.0) + jnp.log1p(jnp.exp(-jnp.abs(x)))


def _layer_norm(y, g, b):
    mu = jnp.mean(y, axis=-1, keepdims=True)
    yc = y - mu
    var = jnp.mean(yc * yc, axis=-1, keepdims=True)
    return yc * lax.rsqrt(var + LN_EPS) * g + b


def _dot(a, b):
    return jnp.dot(a.astype(BF16), b.astype(BF16), preferred_element_type=F32)


def _dot_nt(a, b):
    return lax.dot_general(a.astype(BF16), b.astype(BF16), (((1,), (1,)), ((), ())), preferred_element_type=F32)


def _dot_tn(a, b):
    return lax.dot_general(a.astype(BF16), b.astype(BF16), (((0,), (0,)), ((), ())), preferred_element_type=F32)


def _split(a):
    hi = a.astype(BF16)
    lo = (a - hi.astype(F32)).astype(BF16)
    return hi, lo


def _dot3(a, b):
    a_hi, a_lo = _split(a)
    b_hi, b_lo = _split(b)
    lhs = jnp.concatenate([a_hi, a_lo, a_hi], axis=1)
    rhs = jnp.concatenate([b_hi, b_hi, b_lo], axis=0)
    return jnp.dot(lhs, rhs, preferred_element_type=F32)


def _dot2_exact_rhs(a, b_bf16):
    a_hi, a_lo = _split(a)
    d = functools.partial(jnp.dot, preferred_element_type=F32)
    return d(a_hi, b_bf16) + d(a_lo, b_bf16)


def _iota(shape, dim):
    return lax.broadcasted_iota(jnp.int32, shape, dim)


def _params(*sem):
    return pltpu.CompilerParams(dimension_semantics=sem, vmem_limit_bytes=VMEM_LIMIT)


def _const_spec(shape):
    nd = len(shape)
    return pl.BlockSpec(shape, lambda *_: (0,) * nd, pipeline_mode=pl.Buffered(1))


def _row_spec(tm, width):
    return pl.BlockSpec((tm, width), lambda i: (i, 0))


def _row_tile(rows):
    return ROW_TILE if rows % ROW_TILE == 0 else rows


FFN_SPLIT = 2


def _ffn_ln_kernel(x_ref, wg_ref, wu_ref, wd_ref, g_ref, b_ref, o_ref):
    x = x_ref[...]
    xb = x.astype(BF16)
    fc = D_FF // FFN_SPLIT
    acc = None
    for c in range(FFN_SPLIT):
        sl = slice(c * fc, (c + 1) * fc)
        gate = jnp.dot(xb, wg_ref[:, sl], preferred_element_type=F32)
        up = jnp.dot(xb, wu_ref[:, sl], preferred_element_type=F32)
        act = (_silu(gate) * up).astype(BF16)
        part = jnp.dot(act, wd_ref[sl, :], preferred_element_type=F32)
        acc = part if acc is None else acc + part
    o_ref[...] = _layer_norm(ALPHA * x + 0.5 * acc, g_ref[...], b_ref[...])


def ffn_ln(x, wg, wu, wd, g, b):
    rows = x.shape[0]
    tm = _row_tile(rows)
    return pl.pallas_call(
        _ffn_ln_kernel,
        grid=(rows // tm,),
        in_specs=[_row_spec(tm, D_MODEL), _const_spec(wg.shape), _const_spec(wu.shape), _const_spec(wd.shape),
                  _const_spec((1, D_MODEL)), _const_spec((1, D_MODEL))],
        out_specs=_row_spec(tm, D_MODEL),
        out_shape=jax.ShapeDtypeStruct((rows, D_MODEL), F32),
        compiler_params=_params("parallel"),
        name="ffn_ln",
    )(x, wg, wu, wd, g, b)


def _proj_kernel(x_ref, *refs, widths):
    n = len(widths)
    w_refs, o_refs = refs[:n], refs[n:]
    xb = x_ref[...].astype(BF16)
    for w_ref, o_ref, width in zip(w_refs, o_refs, widths):
        for c0 in range(0, width, D_MODEL):
            c1 = min(c0 + D_MODEL, width)
            o_ref[:, c0:c1] = jnp.dot(xb, w_ref[:, c0:c1], preferred_element_type=F32)


def proj(x, weights):
    rows = x.shape[0]
    tm = _row_tile(rows)
    widths = tuple(w.shape[1] for w in weights)
    outs = pl.pallas_call(
        functools.partial(_proj_kernel, widths=widths),
        grid=(rows // tm,),
        in_specs=[_row_spec(tm, D_MODEL)] + [_const_spec(w.shape) for w in weights],
        out_specs=[_row_spec(tm, n) for n in widths],
        out_shape=[jax.ShapeDtypeStruct((rows, n), F32) for n in widths],
        compiler_params=_params("parallel"),
        name="proj",
    )(x, *weights)
    return outs


def _proj_res_ln_kernel(a_ref, h_ref, w_ref, bias_ref, g_ref, b_ref, o_ref):
    y = jnp.dot(a_ref[...].astype(BF16), w_ref[...], preferred_element_type=F32) + bias_ref[...]
    o_ref[...] = _layer_norm(ALPHA * h_ref[...] + y, g_ref[...], b_ref[...])


def proj_res_ln(a, h, w, bias, g, b):
    rows = a.shape[0]
    tm = _row_tile(rows)
    return pl.pallas_call(
        _proj_res_ln_kernel,
        grid=(rows // tm,),
        in_specs=[_row_spec(tm, a.shape[1]), _row_spec(tm, D_MODEL), _const_spec(w.shape),
                  _const_spec((1, D_MODEL)), _const_spec((1, D_MODEL)), _const_spec((1, D_MODEL))],
        out_specs=_row_spec(tm, D_MODEL),
        out_shape=jax.ShapeDtypeStruct((rows, D_MODEL), F32),
        compiler_params=_params("parallel"),
        name="proj_res_ln",
    )(a, h, w, bias, g, b)


def _gdn_out_ln_kernel(o_ref, z_ref, h_ref, nw_ref, w_ref, g_ref, b_ref, out_ref):
    z = z_ref[...]
    gate = _silu(z)
    parts = []
    for hd in range(GDN_HEADS):
        sl = slice(hd * GDN_DV, (hd + 1) * GDN_DV)
        o = o_ref[:, sl]
        ms = jnp.mean(o * o, axis=-1, keepdims=True)
        parts.append((o * lax.rsqrt(ms + RMS_EPS) * nw_ref[...] * gate[:, sl]).astype(BF16))
    a = jnp.concatenate(parts, axis=-1)
    y = jnp.dot(a, w_ref[...], preferred_element_type=F32)
    out_ref[...] = _layer_norm(ALPHA * h_ref[...] + y, g_ref[...], b_ref[...])


def gdn_out_ln(o, z, h, norm_w, w_out, g, b):
    rows = o.shape[0]
    tm = _row_tile(rows)
    return pl.pallas_call(
        _gdn_out_ln_kernel,
        grid=(rows // tm,),
        in_specs=[_row_spec(tm, GDN_Z), _row_spec(tm, GDN_Z), _row_spec(tm, D_MODEL), _const_spec((1, GDN_DV)),
                  _const_spec(w_out.shape), _const_spec((1, D_MODEL)), _const_spec((1, D_MODEL))],
        out_specs=_row_spec(tm, D_MODEL),
        out_shape=jax.ShapeDtypeStruct((rows, D_MODEL), F32),
        compiler_params=_params("parallel"),
        name="gdn_out_ln",
    )(o, z, h, norm_w, w_out, g, b)


def _glu_kernel(x_ref, w_ref, bias_ref, o_ref):
    xb = x_ref[...].astype(BF16)
    a = jnp.dot(xb, w_ref[:, :D_MODEL], preferred_element_type=F32) + bias_ref[:, :D_MODEL]
    gate = jnp.dot(xb, w_ref[:, D_MODEL:], preferred_element_type=F32) + bias_ref[:, D_MODEL:]
    o_ref[...] = a * _sigmoid(gate)


def glu_proj(x, w, bias):
    rows = x.shape[0]
    tm = _row_tile(rows)
    return pl.pallas_call(
        _glu_kernel,
        grid=(rows // tm,),
        in_specs=[_row_spec(tm, D_MODEL), _const_spec(w.shape), _const_spec((1, 2 * D_MODEL))],
        out_specs=_row_spec(tm, D_MODEL),
        out_shape=jax.ShapeDtypeStruct((rows, D_MODEL), F32),
        compiler_params=_params("parallel"),
        name="glu_proj",
    )(x, w, bias)


def _halo_spec(ts, halo, width):
    per = ts // halo
    return pl.BlockSpec((None, halo, width), lambda b, t: (b, jnp.maximum(t * per - 1, 0), 0))


def _causal_taps(xx_ref, halo, rows, taps, w_ref, col):
    acc = None
    for k in range(taps):
        term = xx_ref[pl.ds(halo - k, rows), col] * w_ref[taps - 1 - k:taps - k, col]
        acc = term if acc is None else acc + term
    return acc


def _gdn_act_kernel(x_ref, halo_ref, ba_ref, cw_ref, alog_ref, dtb_ref, qkv_ref, bg_ref, xx_ref, *, ts, t_real):
    t = pl.program_id(1)
    halo = halo_ref[...]
    xx_ref[0:SUBLANES, :] = jnp.where(t == 0, 0.0, halo)
    xx_ref[SUBLANES:, :] = x_ref[...]
    hk = GDN_HEADS * GDN_DK
    for c in range(GDN_QKV // LANES):
        col = slice(c * LANES, (c + 1) * LANES)
        y = _silu(_causal_taps(xx_ref, SUBLANES, ts, GDN_CONV, cw_ref, col))
        if c * LANES < 2 * hk:
            y = y * lax.rsqrt(jnp.sum(y * y, axis=-1, keepdims=True) + RMS_EPS)
            if c * LANES < hk:
                y = y * GDN_DK ** -0.5
        qkv_ref[:, col] = y
    ba = ba_ref[...]
    row = t * ts + _iota(ba.shape, 0)
    lane = _iota(ba.shape, 1)
    beta = _sigmoid(ba)
    g = -jnp.exp(alog_ref[...]) * _softplus(ba + dtb_ref[...])
    bg_ref[...] = jnp.where(row < t_real, jnp.where(lane < GDN_HEADS, beta, g), 0.0)


def gdn_act(qkv_pre, ba, conv_w, a_log_row, dt_bias_row, t_real):
    nb, t_pad, _ = qkv_pre.shape
    ts = SEQ_TILE
    kern = functools.partial(_gdn_act_kernel, ts=ts, t_real=t_real)
    seq_spec = lambda width: pl.BlockSpec((None, ts, width), lambda b, t: (b, t, 0))
    return pl.pallas_call(
        kern,
        grid=(nb, t_pad // ts),
        in_specs=[seq_spec(GDN_QKV), _halo_spec(ts, SUBLANES, GDN_QKV), seq_spec(LANES),
                  _const_spec(conv_w.shape), _const_spec((1, LANES)), _const_spec((1, LANES))],
        out_specs=[seq_spec(GDN_QKV), seq_spec(LANES)],
        out_shape=[jax.ShapeDtypeStruct((nb, t_pad, GDN_QKV), F32), jax.ShapeDtypeStruct((nb, t_pad, LANES), F32)],
        scratch_shapes=[pltpu.VMEM((ts + SUBLANES, GDN_QKV), F32)],
        compiler_params=_params("parallel", "parallel"),
        name="gdn_act",
    )(qkv_pre, qkv_pre, ba, conv_w, a_log_row, dt_bias_row)


def _cv_conv_kernel(u_ref, halo_ref, w_ref, wb_ref, g_ref, b_ref, o_ref, xx_ref, ys_ref, *, ts):
    t = pl.program_id(1)
    xx_ref[0:CONV_HALO, :] = jnp.where(t == 0, 0.0, halo_ref[...])
    xx_ref[CONV_HALO:, :] = u_ref[...]
    n_keep = ts + CONV_HALO - SUBLANES
    for b in range(1, SUBLANES):
        ys_ref[b - 1, pl.ds(SUBLANES, n_keep), :] = xx_ref[pl.ds(SUBLANES - b, n_keep), :]
    acc = None
    for k in range(CONV_WIDTH):
        a, b = divmod(k, SUBLANES)
        start = CONV_HALO - SUBLANES * a
        x_k = xx_ref[pl.ds(start, ts), :] if b == 0 else ys_ref[b - 1, pl.ds(start, ts), :]
        term = x_k * w_ref[CONV_WIDTH - 1 - k:CONV_WIDTH - k, :]
        acc = term if acc is None else acc + term
    o_ref[...] = _silu(_layer_norm(acc + wb_ref[...], g_ref[...], b_ref[...]))


def cv_conv(u, dw_w, dw_b, ln_g, ln_b):
    nb, t_pad, _ = u.shape
    ts = SEQ_TILE
    seq_spec = pl.BlockSpec((None, ts, D_MODEL), lambda b, t: (b, t, 0))
    return pl.pallas_call(
        functools.partial(_cv_conv_kernel, ts=ts),
        grid=(nb, t_pad // ts),
        in_specs=[seq_spec, _halo_spec(ts, CONV_HALO, D_MODEL), _const_spec(dw_w.shape),
                  _const_spec((1, D_MODEL)), _const_spec((1, D_MODEL)), _const_spec((1, D_MODEL))],
        out_specs=seq_spec,
        out_shape=jax.ShapeDtypeStruct((nb, t_pad, D_MODEL), F32),
        scratch_shapes=[pltpu.VMEM((ts + CONV_HALO, D_MODEL), F32),
                        pltpu.VMEM((SUBLANES - 1, ts + CONV_HALO, D_MODEL), F32)],
        compiler_params=_params("parallel", "parallel"),
        name="cv_conv",
    )(u, u, dw_w, dw_b, ln_g, ln_b)


def _gdn_act_sample_kernel(seq_ref, ba_ref, cw_ref, alog_ref, dtb_ref, qkv_ref, bg_ref):
    hk = GDN_HEADS * GDN_DK
    for c in range(GDN_QKV // LANES):
        col = slice(c * LANES, (c + 1) * LANES)
        acc = None
        for i in range(GDN_CONV):
            term = seq_ref[i, :, col] * cw_ref[i:i + 1, col]
            acc = term if acc is None else acc + term
        y = _silu(acc)
        if c * LANES < 2 * hk:
            y = y * lax.rsqrt(jnp.sum(y * y, axis=-1, keepdims=True) + RMS_EPS)
            if c * LANES < hk:
                y = y * GDN_DK ** -0.5
        qkv_ref[:, col] = y
    ba = ba_ref[...]
    lane = _iota(ba.shape, 1)
    g = -jnp.exp(alog_ref[...]) * _softplus(ba + dtb_ref[...])
    bg_ref[...] = jnp.where(lane < GDN_HEADS, _sigmoid(ba), g)


def gdn_act_sample(seq, ba, conv_w, a_log_row, dt_bias_row):
    n = seq.shape[1]
    return pl.pallas_call(
        _gdn_act_sample_kernel,
        out_shape=[jax.ShapeDtypeStruct((n, GDN_QKV), F32), jax.ShapeDtypeStruct((n, LANES), F32)],
        compiler_params=pltpu.CompilerParams(vmem_limit_bytes=VMEM_LIMIT),
        name="gdn_act_sample",
    )(seq, ba, conv_w, a_log_row, dt_bias_row)


def _cv_conv_sample_kernel(seq_ref, w_ref, wb_ref, g_ref, b_ref, o_ref):
    acc = None
    for i in range(CONV_WIDTH):
        term = seq_ref[i] * w_ref[i:i + 1, :]
        acc = term if acc is None else acc + term
    o_ref[...] = _silu(_layer_norm(acc + wb_ref[...], g_ref[...], b_ref[...]))


def cv_conv_sample(seq, dw_w, dw_b, ln_g, ln_b):
    n = seq.shape[1]
    return pl.pallas_call(
        _cv_conv_sample_kernel,
        out_shape=jax.ShapeDtypeStruct((n, D_MODEL), F32),
        compiler_params=pltpu.CompilerParams(vmem_limit_bytes=VMEM_LIMIT),
        name="cv_conv_sample",
    )(seq, dw_w, dw_b, ln_g, ln_b)


GDN_CHUNKS_PER_STEP = 2


def _gdn_chunk_kernel(qkv_ref, bg_ref, o_ref, s_out_ref, s_ref):
    c = pl.program_id(1)
    n_c = pl.num_programs(1)
    cs = GDN_CHUNK
    n_sub = GDN_CHUNKS_PER_STEP
    hk = GDN_HEADS * GDN_DK

    @pl.when(c == 0)
    def _():
        s_ref[...] = jnp.zeros_like(s_ref)

    row = _iota((cs, LANES), 0)
    ri = _iota((cs, cs), 0)
    ci = _iota((cs, cs), 1)
    lower = ri >= ci
    strict = ri > ci
    eye = (ri == ci).astype(F32)

    gcs, gc_ts = [], []
    for j in range(n_sub):
        gc = bg_ref[j * cs:(j + 1) * cs, :]
        shift = 1
        while shift < cs:
            gc = gc + jnp.where(row >= shift, pltpu.roll(gc, shift, axis=0), 0.0)
            shift *= 2
        gcs.append(gc)
        gc_ts.append(gc.T)

    items = [(j, hd) for j in range(n_sub) for hd in range(GDN_HEADS)]
    idx = range(len(items))
    rows = [slice(j * cs, (j + 1) * cs) for j, _ in items]
    q = [qkv_ref[rows[i], hd * GDN_DK:(hd + 1) * GDN_DK] for i, (j, hd) in enumerate(items)]
    k = [qkv_ref[rows[i], hk + hd * GDN_DK:hk + (hd + 1) * GDN_DK] for i, (j, hd) in enumerate(items)]
    v = [qkv_ref[rows[i], 2 * hk + hd * GDN_DV:2 * hk + (hd + 1) * GDN_DV] for i, (j, hd) in enumerate(items)]
    beta = [bg_ref[rows[i], hd:hd + 1] for i, (j, hd) in enumerate(items)]
    g_col = [gcs[j][:, GDN_HEADS + hd:GDN_HEADS + hd + 1] for j, hd in items]
    g_row = [gc_ts[j][GDN_HEADS + hd:GDN_HEADS + hd + 1, :] for j, hd in items]
    g_last = [gcs[j][cs - 1:cs, GDN_HEADS + hd:GDN_HEADS + hd + 1] for j, hd in items]
    decay = [jnp.exp(jnp.where(lower, g_col[i] - g_row[i], -jnp.inf)) for i in idx]
    eg = [jnp.exp(g_col[i]) for i in idx]
    kb = [k[i] * beta[i] for i in idx]

    kq = [_dot_nt(jnp.concatenate([kb[i], q[i]], axis=0), k[i]) for i in idx]
    qkd = [kq[i][cs:] * decay[i] for i in idx]
    p = [-jnp.where(strict, kq[i][:cs] * decay[i], 0.0) for i in idx]
    t_inv = [eye + p[i] for i in idx]
    p = [_dot3(p[i], p[i]) for i in idx]
    span = 2
    while span < cs:
        m = [_dot3(p[i], jnp.concatenate([t_inv[i], p[i]], axis=1)) for i in idx]
        t_inv = [t_inv[i] + m[i][:, :cs] for i in idx]
        p = [m[i][:, cs:] for i in idx]
        span *= 2
    sol = [_dot3(t_inv[i], jnp.concatenate([v[i] * beta[i], kb[i] * eg[i]], axis=1)) for i in idx]
    qe = [q[i] * eg[i] for i in idx]
    kd = [k[i] * jnp.exp(g_last[i] - g_col[i]) for i in idx]
    e_last = [jnp.exp(g_last[i]) for i in idx]

    s = [s_ref[hd] for hd in range(GDN_HEADS)]
    for j in range(n_sub):
        sub = [j * GDN_HEADS + hd for hd in range(GDN_HEADS)]
        wq = [_dot(jnp.concatenate([sol[i][:, GDN_DV:], qe[i]], axis=0), s[hd]) for hd, i in enumerate(sub)]
        v_new = [sol[i][:, :GDN_DV] - wq[hd][:cs] for hd, i in enumerate(sub)]
        o_in = [_dot(qkd[i], v_new[hd]) for hd, i in enumerate(sub)]
        s_add = [_dot_tn(kd[i], v_new[hd]) for hd, i in enumerate(sub)]
        for hd, i in enumerate(sub):
            o_ref[rows[i], hd * GDN_DV:(hd + 1) * GDN_DV] = wq[hd][cs:] + o_in[hd]
        s = [s[hd] * e_last[i] + s_add[hd] for hd, i in enumerate(sub)]
    for hd in range(GDN_HEADS):
        s_ref[hd] = s[hd]

    @pl.when(c == n_c - 1)
    def _():
        s_out_ref[...] = s_ref[...]


def gdn_chunks(qkv, bg):
    nb, t_pad, _ = qkv.shape
    rows = GDN_CHUNK * GDN_CHUNKS_PER_STEP
    return pl.pallas_call(
        _gdn_chunk_kernel,
        grid=(nb, t_pad // rows),
        in_specs=[pl.BlockSpec((None, rows, GDN_QKV), lambda b, c: (b, c, 0)),
                  pl.BlockSpec((None, rows, LANES), lambda b, c: (b, c, 0))],
        out_specs=[pl.BlockSpec((None, rows, GDN_Z), lambda b, c: (b, c, 0)),
                   pl.BlockSpec((None, GDN_HEADS, GDN_DK, GDN_DV), lambda b, c: (b, 0, 0, 0))],
        out_shape=[jax.ShapeDtypeStruct((nb, t_pad, GDN_Z), F32),
                   jax.ShapeDtypeStruct((nb, GDN_HEADS, GDN_DK, GDN_DV), F32)],
        scratch_shapes=[pltpu.VMEM((GDN_HEADS, GDN_DK, GDN_DV), F32)],
        compiler_params=_params("parallel", "arbitrary"),
        name="gdn_chunks",
    )(qkv, bg)


GDN_STEP_GROUP = 4


def _gdn_step_kernel(s_ref, qk_ref, v_ref, bg_ref, o_ref, s_out_ref, *, group):
    for i in range(group):
        qk_t = qk_ref[i].T
        for hd in range(GDN_HEADS):
            beta = bg_ref[i, :, hd:hd + 1]
            g = bg_ref[i, :, GDN_HEADS + hd:GDN_HEADS + hd + 1]
            q_col = qk_t[:, hd:hd + 1]
            k_col = qk_t[:, GDN_HEADS + hd:GDN_HEADS + hd + 1]
            v_row = v_ref[i, :, hd * GDN_DV:(hd + 1) * GDN_DV]
            s = s_ref[i, hd] * jnp.exp(g)
            ks = jnp.sum(s * k_col, axis=0, keepdims=True)
            delta = (v_row - ks) * beta
            s = s + k_col * delta
            s_out_ref[i, hd] = s
            o_ref[i, :, hd * GDN_DV:(hd + 1) * GDN_DV] = jnp.sum(s * q_col, axis=0, keepdims=True)


def gdn_step(s0, qk, v, bg):
    n = s0.shape[0]
    group = math.gcd(n, GDN_STEP_GROUP)
    s_spec = pl.BlockSpec((group, GDN_HEADS, GDN_DK, GDN_DV), lambda i: (i, 0, 0, 0))
    return pl.pallas_call(
        functools.partial(_gdn_step_kernel, group=group),
        grid=(n // group,),
        in_specs=[s_spec,
                  pl.BlockSpec((group, 2 * GDN_HEADS, GDN_DK), lambda i: (i, 0, 0)),
                  pl.BlockSpec((group, 1, GDN_Z), lambda i: (i, 0, 0)),
                  pl.BlockSpec((group, 1, LANES), lambda i: (i, 0, 0))],
        out_specs=[pl.BlockSpec((group, 1, GDN_Z), lambda i: (i, 0, 0)), s_spec],
        out_shape=[jax.ShapeDtypeStruct((n, 1, GDN_Z), F32),
                   jax.ShapeDtypeStruct((n, GDN_HEADS, GDN_DK, GDN_DV), F32)],
        compiler_params=_params("parallel"),
        name="gdn_step",
    )(s0, qk, v, bg)


SB_HEAD_GROUP = 4


def _sb_prompt_kernel(bias_ref, q_ref, k_ref, v_ref, o_ref):
    hg = pl.program_id(1)
    qi = pl.program_id(2)
    tq, tk, dh = SB_TQ, SB_TK, SB_DH
    n_sub = tq // tk
    heads = range(SB_HEAD_GROUP)
    bias = [bias_ref[hg * SB_HEAD_GROUP + h] * LOG2_E for h in heads]
    q = [(q_ref[:, h * dh:(h + 1) * dh] * (dh ** -0.5 * LOG2_E)).astype(BF16) for h in heads]
    tr = _iota((2 * tk, 2 * tk), 0) & (tk - 1)
    tc = _iota((2 * tk, 2 * tk), 1)
    tail = jnp.where((tc >= tk) | (tr > tc), -1.0, 0.0).astype(BF16)
    below_diag = _iota((tq, tq), 1) < _iota((tq, tq), 0)

    def tile(k0, masked, acc, run):
        kblk = [k_ref[pl.ds(k0, tq), h * dh:(h + 1) * dh].astype(BF16) for h in heads]
        vblk = [v_ref[pl.ds(k0, tq), h * dh:(h + 1) * dh].astype(BF16) for h in heads]
        z = [lax.dot_general(q[h], kblk[h], (((1,), (1,)), ((), ())), preferred_element_type=F32) + bias[h]
             for h in heads]
        z_pos = [jnp.maximum(z[h], 0.0) for h in heads]
        z_neg = [jnp.minimum(z[h], 0.0) for h in heads]
        lg = [jnp.log(1.0 + jnp.exp2(z_neg[h] - z_pos[h])) * LOG2_E for h in heads]
        sp = [z_pos[h] + lg[h] for h in heads]
        log_beta = [z_neg[h] - lg[h] for h in heads]
        if masked:
            sp = [jnp.where(below_diag, sp[h], 0.0) for h in heads]
        split = [_split(sp[h]) for h in heads]
        res = [[jnp.dot(jnp.concatenate([part[:, s * tk:(s + 1) * tk] for part in split[h]], axis=1),
                        tail, preferred_element_type=F32) for s in range(n_sub)] for h in heads]
        acc, run = list(acc), list(run)
        for h in heads:
            w = [None] * n_sub
            for s in reversed(range(n_sub)):
                w[s] = jnp.exp2(log_beta[h][:, s * tk:(s + 1) * tk] + (res[h][s][:, :tk] + run[h]))
                run[h] = run[h] + res[h][s][:, tk:]
            w = jnp.concatenate(w, axis=1)
            if masked:
                w = jnp.where(below_diag, w, 0.0)
            acc[h] = acc[h] + jnp.dot(w.astype(BF16), vblk[h], preferred_element_type=F32)
        return tuple(acc), tuple(run)

    zeros = lambda width: tuple(jnp.zeros((tq, width), F32) for _ in heads)
    acc, run = tile(pl.multiple_of(qi * tq, tq), True, zeros(dh), zeros(tk))

    def body(i, carry):
        return tile(pl.multiple_of((qi - 1 - i) * tq, tq), False, *carry)

    acc, _ = lax.fori_loop(0, qi, body, (acc, run))
    for h in heads:
        o_ref[:, h * dh:(h + 1) * dh] = acc[h]


def sb_prompt_attn(qkv, bias):
    nb, t_pad, _ = qkv.shape
    tq = SB_TQ
    gw = SB_HEAD_GROUP * SB_DH
    n_groups = SB_HEADS // SB_HEAD_GROUP
    grid_spec = pltpu.PrefetchScalarGridSpec(
        num_scalar_prefetch=0,
        grid=(nb, n_groups, t_pad // tq),
        in_specs=[pl.BlockSpec(memory_space=pltpu.SMEM),
                  pl.BlockSpec((None, tq, gw), lambda b, g, i: (b, i, g)),
                  pl.BlockSpec((None, t_pad, gw), lambda b, g, i: (b, 0, n_groups + g)),
                  pl.BlockSpec((None, t_pad, gw), lambda b, g, i: (b, 0, 2 * n_groups + g))],
        out_specs=pl.BlockSpec((None, tq, gw), lambda b, g, i: (b, i, g)),
    )
    return pl.pallas_call(
        _sb_prompt_kernel,
        grid_spec=grid_spec,
        out_shape=jax.ShapeDtypeStruct((nb, t_pad, D_MODEL), F32),
        compiler_params=_params("parallel", "parallel", "arbitrary"),
        name="sb_prompt_attn",
    )(bias, qkv, qkv, qkv)


SB_PAGES_PER_STEP = 8


def _sb_decode_kernel(pt_ref, q_ref, bias_ref, expand_ref, *refs, n_pg):
    k_refs, v_refs = refs[:n_pg], refs[n_pg:2 * n_pg]
    o_ref, acc_ref, run_ref = refs[2 * n_pg:]
    step = pl.program_id(1)
    n_steps = pl.num_programs(1)
    ps = PAGE_SIZE
    pages = range(n_pg)

    @pl.when(step == 0)
    def _():
        acc_ref[...] = jnp.zeros_like(acc_ref)
        run_ref[...] = jnp.zeros_like(run_ref)

    q_mat = q_ref[...]
    lane = _iota((ps, LANES), 1)
    z = []
    for i in pages:
        zi = jnp.zeros((ps, LANES), F32)
        for hd in range(SB_HEADS):
            k_head = k_refs[i][pl.ds(hd, ps, stride=SB_HEADS), :]
            zh = jnp.dot(k_head.astype(BF16), q_mat, preferred_element_type=F32)
            zi = jnp.where(lane == hd, zh, zi)
        z.append(zi * SB_DH ** -0.5 + bias_ref[...])
    lg = [jnp.log(1.0 + jnp.exp(-jnp.abs(z[i]))) for i in pages]
    sp = [jnp.maximum(z[i], 0.0) + lg[i] for i in pages]
    log_beta = [jnp.minimum(z[i], 0.0) - lg[i] for i in pages]
    nr = _iota((ps, 2 * ps), 0)
    nc = _iota((ps, 2 * ps), 1) & (ps - 1)
    newer = jnp.where(nc > nr, -1.0, 0.0).astype(BF16)
    after = [jnp.dot(newer, jnp.concatenate(_split(sp[i]), axis=0), preferred_element_type=F32) for i in pages]
    run = run_ref[...]
    w = []
    for i in pages:
        w.append(jnp.exp(log_beta[i] + after[i] + run).astype(BF16))
        run = run + (after[i][0:1, :] - sp[i][0:1, :])
    run_ref[...] = run
    w_wide = [jnp.dot(w[i], expand_ref[...], preferred_element_type=F32) for i in pages]
    for hd in range(SB_HEADS):
        part = None
        for i in pages:
            v_head = v_refs[i][pl.ds(hd, ps, stride=SB_HEADS), :]
            prod = w_wide[i][:, hd * SB_DH:(hd + 1) * SB_DH] * v_head
            prod = jnp.sum(prod.reshape(ps // SUBLANES, SUBLANES, SB_DH), axis=0)
            part = prod if part is None else part + prod
        acc_ref[hd] += part

    @pl.when(step == n_steps - 1)
    def _():
        for hd in range(SB_HEADS):
            o_ref[:, hd * SB_DH:(hd + 1) * SB_DH] = jnp.sum(acc_ref[hd], axis=0, keepdims=True)


def sb_decode_attn(q_mat, bias_row, expand, cache_k, cache_v, page_table):
    n, n_pages = page_table.shape
    n_pg = math.gcd(n_pages, SB_PAGES_PER_STEP)

    def page_spec(i):
        return pl.BlockSpec((None, PAGE_SIZE * SB_HEADS, SB_DH),
                            lambda s, p, pt: (pt[s, n_pages - 1 - (p * n_pg + i)], 0, 0))

    grid_spec = pltpu.PrefetchScalarGridSpec(
        num_scalar_prefetch=1,
        grid=(n, n_pages // n_pg),
        in_specs=[pl.BlockSpec((None, SB_DH, LANES), lambda s, p, pt: (s, 0, 0)),
                  pl.BlockSpec((1, LANES), lambda s, p, pt: (0, 0)),
                  pl.BlockSpec((LANES, D_MODEL), lambda s, p, pt: (0, 0))]
                 + [page_spec(i) for i in range(n_pg)] * 2,
        out_specs=pl.BlockSpec((None, 1, D_MODEL), lambda s, p, pt: (s, 0, 0)),
        scratch_shapes=[pltpu.VMEM((SB_HEADS, SUBLANES, SB_DH), F32), pltpu.VMEM((1, LANES), F32)],
    )
    return pl.pallas_call(
        functools.partial(_sb_decode_kernel, n_pg=n_pg),
        grid_spec=grid_spec,
        out_shape=jax.ShapeDtypeStruct((n, 1, D_MODEL), F32),
        compiler_params=_params("parallel", "arbitrary"),
        name="sb_decode_attn",
    )(page_table, q_mat, bias_row, expand, *([cache_k] * n_pg), *([cache_v] * n_pg))


def _row(v):
    return v.reshape(1, -1).astype(F32)


def _pad_lanes(v, offset):
    out = jnp.zeros((1, LANES), F32)
    return lax.dynamic_update_slice(out, v.reshape(1, -1).astype(F32), (0, offset))


def kernel(x_prompt, x_sample, state_gdn_conv, state_gdn_S, cache_sb_k, cache_sb_v, state_conv, page_table,
           meta_tokens, ln_g, ln_b, ffn_w_gate, ffn_w_up, ffn_w_down,
           gdn_w_in, gdn_conv_w, gdn_a_log, gdn_dt_bias, gdn_norm_w, gdn_w_out,
           sb_w_qkv, sb_w_out, sb_logit_bias,
           cv_w_pw1, cv_b_pw1, cv_dw_w, cv_dw_b, cv_ln_g, cv_ln_b, cv_w_pw2, cv_b_pw2):
    nb, seq, _ = x_prompt.shape
    ns = x_sample.shape[0]
    assert x_sample.shape[1] == 1, "the sample group holds one new token per sequence"
    t_real = N_META + seq
    t_pad = -(-t_real // SEQ_ALIGN) * SEQ_ALIGN
    rows_p = nb * t_pad

    meta = jnp.broadcast_to(meta_tokens.astype(F32)[None], (nb, N_META, D_MODEL))
    hp = jnp.concatenate([meta, x_prompt, jnp.zeros((nb, t_pad - t_real, D_MODEL), F32)], axis=1)
    hp = hp.reshape(rows_p, D_MODEL)
    hs = x_sample.reshape(ns, D_MODEL)

    gdn_s_p, gdn_s_s, gdn_c_p, gdn_c_s = [], [], [], []
    sb_k_p, sb_v_p, sb_k_s, sb_v_s = [], [], [], []
    cv_p, cv_s = [], []
    zero_bias = jnp.zeros((1, D_MODEL), F32)

    def ffn(h, i, half, ln_idx):
        return ffn_ln(h, ffn_w_gate[i, half].astype(BF16), ffn_w_up[i, half].astype(BF16),
                      ffn_w_down[i, half].astype(BF16), _row(ln_g[i, ln_idx]), _row(ln_b[i, ln_idx]))

    for i in range(DEPTH):
        j, kind = i // N_MIXERS, i % N_MIXERS
        hp = ffn(hp, i, 0, 0)
        hs = ffn(hs, i, 0, 0)
        g1, b1 = _row(ln_g[i, 1]), _row(ln_b[i, 1])

        if kind == 0:
            w_in = gdn_w_in[j]
            w_qkv = w_in[:, :GDN_QKV].astype(BF16)
            w_z = w_in[:, GDN_QKV:GDN_QKV + GDN_Z].astype(BF16)
            w_ba = jnp.pad(w_in[:, GDN_QKV + GDN_Z:], ((0, 0), (0, LANES - 2 * GDN_HEADS))).astype(BF16)
            conv_w = gdn_conv_w[j].astype(F32)
            a_log_row = _pad_lanes(gdn_a_log[j], GDN_HEADS)
            dt_bias_row = _pad_lanes(gdn_dt_bias[j], GDN_HEADS)
            norm_w = _row(gdn_norm_w[j])
            w_out = gdn_w_out[j].astype(BF16)

            qkv_pre, z, ba = proj(hp, [w_qkv, w_z, w_ba])
            qkv_pre3 = qkv_pre.reshape(nb, t_pad, GDN_QKV)
            qkv_act, bg = gdn_act(qkv_pre3, ba.reshape(nb, t_pad, LANES), conv_w, a_log_row, dt_bias_row, t_real)
            o, s_fin = gdn_chunks(qkv_act, bg)
            hp = gdn_out_ln(o.reshape(rows_p, GDN_Z), z, hp, norm_w, w_out, g1, b1)
            gdn_s_p.append(s_fin)
            gdn_c_p.append(qkv_pre3[:, t_real - (GDN_CONV - 1):t_real])

            qkv_pre, z, ba = proj(hs, [w_qkv, w_z, w_ba])
            seq_s = jnp.concatenate([state_gdn_conv[j].astype(F32), qkv_pre[:, None]], axis=1)
            qkv_act, bg = gdn_act_sample(jnp.swapaxes(seq_s, 0, 1), ba, conv_w, a_log_row, dt_bias_row)
            hk = GDN_HEADS * GDN_DK
            qk_rows = qkv_act[:, :2 * hk].reshape(ns, 2 * GDN_HEADS, GDN_DK)
            v_row = qkv_act[:, 2 * hk:].reshape(ns, 1, GDN_Z)
            o, s_fin = gdn_step(state_gdn_S[j].astype(F32), qk_rows, v_row, bg.reshape(ns, 1, LANES))
            hs = gdn_out_ln(o.reshape(ns, GDN_Z), z, hs, norm_w, w_out, g1, b1)
            gdn_s_s.append(s_fin)
            gdn_c_s.append(seq_s[:, 1:])

        elif kind == 1:
            w_qkv = sb_w_qkv[j].astype(BF16)
            w_out = sb_w_out[j].astype(BF16)
            bias = sb_logit_bias[j].astype(F32)

            (qkv,) = proj(hp, [w_qkv])
            qkv3 = qkv.reshape(nb, t_pad, 3 * D_MODEL)
            o = sb_prompt_attn(qkv3, bias)
            hp = proj_res_ln(o.reshape(rows_p, D_MODEL), hp, w_out, zero_bias, g1, b1)
            sb_k_p.append(qkv3[:, :t_real, D_MODEL:2 * D_MODEL].reshape(nb, t_real, SB_HEADS, SB_DH))
            sb_v_p.append(qkv3[:, :t_real, 2 * D_MODEL:].reshape(nb, t_real, SB_HEADS, SB_DH))

            (qkv,) = proj(hs, [w_qkv])
            q_t = jnp.swapaxes(qkv[:, :D_MODEL].reshape(ns, SB_HEADS, SB_DH), 1, 2)
            q_mat = jnp.pad(q_t, ((0, 0), (0, 0), (0, LANES - SB_HEADS))).astype(BF16)
            head_of_lane = jnp.arange(D_MODEL) // SB_DH
            expand = (jnp.arange(LANES)[:, None] == head_of_lane[None, :]).astype(BF16)
            bias_row = _pad_lanes(bias, 0)
            n_phys = cache_sb_k.shape[1]
            o = sb_decode_attn(q_mat, bias_row, expand,
                               cache_sb_k.reshape(-1, PAGE_SIZE * SB_HEADS, SB_DH).astype(F32),
                               cache_sb_v.reshape(-1, PAGE_SIZE * SB_HEADS, SB_DH).astype(F32),
                               page_table.astype(jnp.int32) + j * n_phys)
            hs = proj_res_ln(o.reshape(ns, D_MODEL), hs, w_out, zero_bias, g1, b1)
            sb_k_s.append(qkv[:, D_MODEL:2 * D_MODEL].reshape(ns, 1, SB_HEADS, SB_DH))
            sb_v_s.append(qkv[:, 2 * D_MODEL:].reshape(ns, 1, SB_HEADS, SB_DH))

        else:
            w1 = cv_w_pw1[j].astype(BF16)
            b_pw1 = _row(cv_b_pw1[j])
            dw_w = cv_dw_w[j].astype(F32)
            dw_b, cg, cb = _row(cv_dw_b[j]), _row(cv_ln_g[j]), _row(cv_ln_b[j])
            w2 = cv_w_pw2[j].astype(BF16)
            b_pw2 = _row(cv_b_pw2[j])

            u = glu_proj(hp, w1, b_pw1).reshape(nb, t_pad, D_MODEL)
            d = cv_conv(u, dw_w, dw_b, cg, cb)
            hp = proj_res_ln(d.reshape(rows_p, D_MODEL), hp, w2, b_pw2, g1, b1)
            cv_p.append(u[:, t_real - (CONV_WIDTH - 1):t_real])

            u = glu_proj(hs, w1, b_pw1)
            seq_s = jnp.concatenate([state_conv[j].astype(F32), u[:, None]], axis=1)
            d = cv_conv_sample(jnp.swapaxes(seq_s, 0, 1), dw_w, dw_b, cg, cb)
            hs = proj_res_ln(d, hs, w2, b_pw2, g1, b1)
            cv_s.append(seq_s[:, 1:])

        hp = ffn(hp, i, 1, 2)
        hs = ffn(hs, i, 1, 2)

    y_prompt = hp.reshape(nb, t_pad, D_MODEL)[:, N_META:t_real]
    y_sample = hs.reshape(ns, 1, D_MODEL)
    return (y_prompt, y_sample,
            jnp.stack(gdn_s_p), jnp.stack(gdn_s_s), jnp.stack(gdn_c_p), jnp.stack(gdn_c_s),
            jnp.stack(sb_k_p), jnp.stack(sb_v_p), jnp.stack(sb_k_s), jnp.stack(sb_v_s),
            jnp.stack(cv_p), jnp.stack(cv_s))
```

```python
import functools
import math

import jax
import jax.numpy as jnp
from jax import lax
from jax.experimental import pallas as pl
from jax.experimental.pallas import tpu as pltpu

F32 = jnp.float32
BF16 = jnp.bfloat16

D_MODEL = 1024
DEPTH = 4
N_MIXERS = 3
N_META = 16
ALPHA = (2 * DEPTH) ** 0.25
D_FF = 2816
LN_EPS = 1e-5
RMS_EPS = 1e-6
PAGE_SIZE = 128
LOG2_E = math.log2(math.e)

GDN_HEADS = 8
GDN_DK = 128
GDN_DV = 128
GDN_CONV = 4
GDN_QKV = GDN_HEADS * (2 * GDN_DK + GDN_DV)
GDN_Z = GDN_HEADS * GDN_DV
GDN_CHUNK = 64

SB_HEADS = 8
SB_DH = 128
SB_TQ = 384
SB_TK = 128

CONV_WIDTH = 31
CONV_HALO = 32

SEQ_ALIGN = 384
ROW_TILE = 512
SEQ_TILE = 384
LANES = 128
SUBLANES = 8
VMEM_LIMIT = 56 * 1024 * 1024


def _sigmoid(x):
    return 0.5 * jnp.tanh(0.5 * x) + 0.5


def _silu(x):
    return x * _sigmoid(x)


def _softplus(x):
    return jnp.maximum(x, 0.0) + jnp.log1p(jnp.exp(-jnp.abs(x)))


def _layer_norm(y, g, b):
    mu = jnp.mean(y, axis=-1, keepdims=True)
    yc = y - mu
    var = jnp.mean(yc * yc, axis=-1, keepdims=True)
    return yc * lax.rsqrt(var + LN_EPS) * g + b


def _dot(a, b):
    return jnp.dot(a.astype(BF16), b.astype(BF16), preferred_element_type=F32)


def _dot_nt(a, b):
    return lax.dot_general(a.astype(BF16), b.astype(BF16), (((1,), (1,)), ((), ())), preferred_element_type=F32)


def _dot_tn(a, b):
    return lax.dot_general(a.astype(BF16), b.astype(BF16), (((0,), (0,)), ((), ())), preferred_element_type=F32)


def _split(a):
    hi = a.astype(BF16)
    lo = (a - hi.astype(F32)).astype(BF16)
    return hi, lo


def _dot3(a, b):
    a_hi, a_lo = _split(a)
    b_hi, b_lo = _split(b)
    lhs = jnp.concatenate([a_hi, a_lo, a_hi], axis=1)
    rhs = jnp.concatenate([b_hi, b_hi, b_lo], axis=0)
    return jnp.dot(lhs, rhs, preferred_element_type=F32)


def _dot2_exact_rhs(a, b_bf16):
    a_hi, a_lo = _split(a)
    d = functools.partial(jnp.dot, preferred_element_type=F32)
    return d(a_hi, b_bf16) + d(a_lo, b_bf16)


def _iota(shape, dim):
    return lax.broadcasted_iota(jnp.int32, shape, dim)


def _params(*sem):
    return pltpu.CompilerParams(dimension_semantics=sem, vmem_limit_bytes=VMEM_LIMIT)


def _const_spec(shape):
    nd = len(shape)
    return pl.BlockSpec(shape, lambda *_: (0,) * nd, pipeline_mode=pl.Buffered(1))


def _row_spec(tm, width):
    return pl.BlockSpec((tm, width), lambda i: (i, 0))


def _row_tile(rows):
    return ROW_TILE if rows % ROW_TILE == 0 else rows


FFN_SPLIT = 2


def _ffn_ln_kernel(x_ref, wg_ref, wu_ref, wd_ref, g_ref, b_ref, o_ref):
    x = x_ref[...]
    xb = x.astype(BF16)
    fc = D_FF // FFN_SPLIT
    acc = None
    for c in range(FFN_SPLIT):
        sl = slice(c * fc, (c + 1) * fc)
        gate = jnp.dot(xb, wg_ref[:, sl], preferred_element_type=F32)
        up = jnp.dot(xb, wu_ref[:, sl], preferred_element_type=F32)
        act = (_silu(gate) * up).astype(BF16)
        part = jnp.dot(act, wd_ref[sl, :], preferred_element_type=F32)
        acc = part if acc is None else acc + part
    o_ref[...] = _layer_norm(ALPHA * x + 0.5 * acc, g_ref[...], b_ref[...])


def ffn_ln(x, wg, wu, wd, g, b):
    rows = x.shape[0]
    tm = _row_tile(rows)
    return pl.pallas_call(
        _ffn_ln_kernel,
        grid=(rows // tm,),
        in_specs=[_row_spec(tm, D_MODEL), _const_spec(wg.shape), _const_spec(wu.shape), _const_spec(wd.shape),
                  _const_spec((1, D_MODEL)), _const_spec((1, D_MODEL))],
        out_specs=_row_spec(tm, D_MODEL),
        out_shape=jax.ShapeDtypeStruct((rows, D_MODEL), F32),
        compiler_params=_params("parallel"),
        name="ffn_ln",
    )(x, wg, wu, wd, g, b)


def _proj_kernel(x_ref, *refs, widths):
    n = len(widths)
    w_refs, o_refs = refs[:n], refs[n:]
    xb = x_ref[...].astype(BF16)
    for w_ref, o_ref, width in zip(w_refs, o_refs, widths):
        for c0 in range(0, width, D_MODEL):
            c1 = min(c0 + D_MODEL, width)
            o_ref[:, c0:c1] = jnp.dot(xb, w_ref[:, c0:c1], preferred_element_type=F32)


def proj(x, weights):
    rows = x.shape[0]
    tm = _row_tile(rows)
    widths = tuple(w.shape[1] for w in weights)
    outs = pl.pallas_call(
        functools.partial(_proj_kernel, widths=widths),
        grid=(rows // tm,),
        in_specs=[_row_spec(tm, D_MODEL)] + [_const_spec(w.shape) for w in weights],
        out_specs=[_row_spec(tm, n) for n in widths],
        out_shape=[jax.ShapeDtypeStruct((rows, n), F32) for n in widths],
        compiler_params=_params("parallel"),
        name="proj",
    )(x, *weights)
    return outs


def _proj_res_ln_kernel(a_ref, h_ref, w_ref, bias_ref, g_ref, b_ref, o_ref):
    y = jnp.dot(a_ref[...].astype(BF16), w_ref[...], preferred_element_type=F32) + bias_ref[...]
    o_ref[...] = _layer_norm(ALPHA * h_ref[...] + y, g_ref[...], b_ref[...])


def proj_res_ln(a, h, w, bias, g, b):
    rows = a.shape[0]
    tm = _row_tile(rows)
    return pl.pallas_call(
        _proj_res_ln_kernel,
        grid=(rows // tm,),
        in_specs=[_row_spec(tm, a.shape[1]), _row_spec(tm, D_MODEL), _const_spec(w.shape),
                  _const_spec((1, D_MODEL)), _const_spec((1, D_MODEL)), _const_spec((1, D_MODEL))],
        out_specs=_row_spec(tm, D_MODEL),
        out_shape=jax.ShapeDtypeStruct((rows, D_MODEL), F32),
        compiler_params=_params("parallel"),
        name="proj_res_ln",
    )(a, h, w, bias, g, b)


def _gdn_out_ln_kernel(o_ref, z_ref, h_ref, nw_ref, w_ref, g_ref, b_ref, out_ref):
    z = z_ref[...]
    gate = _silu(z)
    parts = []
    for hd in range(GDN_HEADS):
        sl = slice(hd * GDN_DV, (hd + 1) * GDN_DV)
        o = o_ref[:, sl]
        ms = jnp.mean(o * o, axis=-1, keepdims=True)
        parts.append((o * lax.rsqrt(ms + RMS_EPS) * nw_ref[...] * gate[:, sl]).astype(BF16))
    a = jnp.concatenate(parts, axis=-1)
    y = jnp.dot(a, w_ref[...], preferred_element_type=F32)
    out_ref[...] = _layer_norm(ALPHA * h_ref[...] + y, g_ref[...], b_ref[...])


def gdn_out_ln(o, z, h, norm_w, w_out, g, b):
    rows = o.shape[0]
    tm = _row_tile(rows)
    return pl.pallas_call(
        _gdn_out_ln_kernel,
        grid=(rows // tm,),
        in_specs=[_row_spec(tm, GDN_Z), _row_spec(tm, GDN_Z), _row_spec(tm, D_MODEL), _const_spec((1, GDN_DV)),
                  _const_spec(w_out.shape), _const_spec((1, D_MODEL)), _const_spec((1, D_MODEL))],
        out_specs=_row_spec(tm, D_MODEL),
        out_shape=jax.ShapeDtypeStruct((rows, D_MODEL), F32),
        compiler_params=_params("parallel"),
        name="gdn_out_ln",
    )(o, z, h, norm_w, w_out, g, b)


def _glu_kernel(x_ref, w_ref, bias_ref, o_ref):
    xb = x_ref[...].astype(BF16)
    a = jnp.dot(xb, w_ref[:, :D_MODEL], preferred_element_type=F32) + bias_ref[:, :D_MODEL]
    gate = jnp.dot(xb, w_ref[:, D_MODEL:], preferred_element_type=F32) + bias_ref[:, D_MODEL:]
    o_ref[...] = a * _sigmoid(gate)


def glu_proj(x, w, bias):
    rows = x.shape[0]
    tm = _row_tile(rows)
    return pl.pallas_call(
        _glu_kernel,
        grid=(rows // tm,),
        in_specs=[_row_spec(tm, D_MODEL), _const_spec(w.shape), _const_spec((1, 2 * D_MODEL))],
        out_specs=_row_spec(tm, D_MODEL),
        out_shape=jax.ShapeDtypeStruct((rows, D_MODEL), F32),
        compiler_params=_params("parallel"),
        name="glu_proj",
    )(x, w, bias)


def _halo_spec(ts, halo, width):
    per = ts // halo
    return pl.BlockSpec((None, halo, width), lambda b, t: (b, jnp.maximum(t * per - 1, 0), 0))


def _causal_taps(xx_ref, halo, rows, taps, w_ref, col):
    acc = None
    for k in range(taps):
        term = xx_ref[pl.ds(halo - k, rows), col] * w_ref[taps - 1 - k:taps - k, col]
        acc = term if acc is None else acc + term
    return acc


def _gdn_act_kernel(x_ref, halo_ref, ba_ref, cw_ref, alog_ref, dtb_ref, qkv_ref, bg_ref, xx_ref, *, ts, t_real):
    t = pl.program_id(1)
    halo = halo_ref[...]
    xx_ref[0:SUBLANES, :] = jnp.where(t == 0, 0.0, halo)
    xx_ref[SUBLANES:, :] = x_ref[...]
    hk = GDN_HEADS * GDN_DK
    for c in range(GDN_QKV // LANES):
        col = slice(c * LANES, (c + 1) * LANES)
        y = _silu(_causal_taps(xx_ref, SUBLANES, ts, GDN_CONV, cw_ref, col))
        if c * LANES < 2 * hk:
            y = y * lax.rsqrt(jnp.sum(y * y, axis=-1, keepdims=True) + RMS_EPS)
            if c * LANES < hk:
                y = y * GDN_DK ** -0.5
        qkv_ref[:, col] = y
    ba = ba_ref[...]
    row = t * ts + _iota(ba.shape, 0)
    lane = _iota(ba.shape, 1)
    beta = _sigmoid(ba)
    g = -jnp.exp(alog_ref[...]) * _softplus(ba + dtb_ref[...])
    bg_ref[...] = jnp.where(row < t_real, jnp.where(lane < GDN_HEADS, beta, g), 0.0)


def gdn_act(qkv_pre, ba, conv_w, a_log_row, dt_bias_row, t_real):
    nb, t_pad, _ = qkv_pre.shape
    ts = SEQ_TILE
    kern = functools.partial(_gdn_act_kernel, ts=ts, t_real=t_real)
    seq_spec = lambda width: pl.BlockSpec((None, ts, width), lambda b, t: (b, t, 0))
    return pl.pallas_call(
        kern,
        grid=(nb, t_pad // ts),
        in_specs=[seq_spec(GDN_QKV), _halo_spec(ts, SUBLANES, GDN_QKV), seq_spec(LANES),
                  _const_spec(conv_w.shape), _const_spec((1, LANES)), _const_spec((1, LANES))],
        out_specs=[seq_spec(GDN_QKV), seq_spec(LANES)],
        out_shape=[jax.ShapeDtypeStruct((nb, t_pad, GDN_QKV), F32), jax.ShapeDtypeStruct((nb, t_pad, LANES), F32)],
        scratch_shapes=[pltpu.VMEM((ts + SUBLANES, GDN_QKV), F32)],
        compiler_params=_params("parallel", "parallel"),
        name="gdn_act",
    )(qkv_pre, qkv_pre, ba, conv_w, a_log_row, dt_bias_row)


def _cv_conv_kernel(u_ref, halo_ref, w_ref, wb_ref, g_ref, b_ref, o_ref, xx_ref, ys_ref, *, ts):
    t = pl.program_id(1)
    xx_ref[0:CONV_HALO, :] = jnp.where(t == 0, 0.0, halo_ref[...])
    xx_ref[CONV_HALO:, :] = u_ref[...]
    n_keep = ts + CONV_HALO - SUBLANES
    for b in range(1, SUBLANES):
        ys_ref[b - 1, pl.ds(SUBLANES, n_keep), :] = xx_ref[pl.ds(SUBLANES - b, n_keep), :]
    acc = None
    for k in range(CONV_WIDTH):
        a, b = divmod(k, SUBLANES)
        start = CONV_HALO - SUBLANES * a
        x_k = xx_ref[pl.ds(start, ts), :] if b == 0 else ys_ref[b - 1, pl.ds(start, ts), :]
        term = x_k * w_ref[CONV_WIDTH - 1 - k:CONV_WIDTH - k, :]
        acc = term if acc is None else acc + term
    o_ref[...] = _silu(_layer_norm(acc + wb_ref[...], g_ref[...], b_ref[...]))


def cv_conv(u, dw_w, dw_b, ln_g, ln_b):
    nb, t_pad, _ = u.shape
    ts = SEQ_TILE
    seq_spec = pl.BlockSpec((None, ts, D_MODEL), lambda b, t: (b, t, 0))
    return pl.pallas_call(
        functools.partial(_cv_conv_kernel, ts=ts),
        grid=(nb, t_pad // ts),
        in_specs=[seq_spec, _halo_spec(ts, CONV_HALO, D_MODEL), _const_spec(dw_w.shape),
                  _const_spec((1, D_MODEL)), _const_spec((1, D_MODEL)), _const_spec((1, D_MODEL))],
        out_specs=seq_spec,
        out_shape=jax.ShapeDtypeStruct((nb, t_pad, D_MODEL), F32),
        scratch_shapes=[pltpu.VMEM((ts + CONV_HALO, D_MODEL), F32),
                        pltpu.VMEM((SUBLANES - 1, ts + CONV_HALO, D_MODEL), F32)],
        compiler_params=_params("parallel", "parallel"),
        name="cv_conv",
    )(u, u, dw_w, dw_b, ln_g, ln_b)


def _gdn_act_sample_kernel(seq_ref, ba_ref, cw_ref, alog_ref, dtb_ref, qkv_ref, bg_ref):
    hk = GDN_HEADS * GDN_DK
    for c in range(GDN_QKV // LANES):
        col = slice(c * LANES, (c + 1) * LANES)
        acc = None
        for i in range(GDN_CONV):
            term = seq_ref[i, :, col] * cw_ref[i:i + 1, col]
            acc = term if acc is None else acc + term
        y = _silu(acc)
        if c * LANES < 2 * hk:
            y = y * lax.rsqrt(jnp.sum(y * y, axis=-1, keepdims=True) + RMS_EPS)
            if c * LANES < hk:
                y = y * GDN_DK ** -0.5
        qkv_ref[:, col] = y
    ba = ba_ref[...]
    lane = _iota(ba.shape, 1)
    g = -jnp.exp(alog_ref[...]) * _softplus(ba + dtb_ref[...])
    bg_ref[...] = jnp.where(lane < GDN_HEADS, _sigmoid(ba), g)


def gdn_act_sample(seq, ba, conv_w, a_log_row, dt_bias_row):
    n = seq.shape[1]
    return pl.pallas_call(
        _gdn_act_sample_kernel,
        out_shape=[jax.ShapeDtypeStruct((n, GDN_QKV), F32), jax.ShapeDtypeStruct((n, LANES), F32)],
        compiler_params=pltpu.CompilerParams(vmem_limit_bytes=VMEM_LIMIT),
        name="gdn_act_sample",
    )(seq, ba, conv_w, a_log_row, dt_bias_row)


def _cv_conv_sample_kernel(seq_ref, w_ref, wb_ref, g_ref, b_ref, o_ref):
    acc = None
    for i in range(CONV_WIDTH):
        term = seq_ref[i] * w_ref[i:i + 1, :]
        acc = term if acc is None else acc + term
    o_ref[...] = _silu(_layer_norm(acc + wb_ref[...], g_ref[...], b_ref[...]))


def cv_conv_sample(seq, dw_w, dw_b, ln_g, ln_b):
    n = seq.shape[1]
    return pl.pallas_call(
        _cv_conv_sample_kernel,
        out_shape=jax.ShapeDtypeStruct((n, D_MODEL), F32),
        compiler_params=pltpu.CompilerParams(vmem_limit_bytes=VMEM_LIMIT),
        name="cv_conv_sample",
    )(seq, dw_w, dw_b, ln_g, ln_b)


GDN_CHUNKS_PER_STEP = 3


def _gdn_chunk_kernel(qkv_ref, bg_ref, o_ref, s_out_ref, s_ref):
    c = pl.program_id(1)
    n_c = pl.num_programs(1)
    cs = GDN_CHUNK
    n_sub = GDN_CHUNKS_PER_STEP
    hk = GDN_HEADS * GDN_DK

    @pl.when(c == 0)
    def _():
        s_ref[...] = jnp.zeros_like(s_ref)

    row = _iota((cs, LANES), 0)
    ri = _iota((cs, cs), 0)
    ci = _iota((cs, cs), 1)
    lower = ri >= ci
    strict = ri > ci
    eye = (ri == ci).astype(F32)

    gcs, gc_ts = [], []
    for j in range(n_sub):
        gc = bg_ref[j * cs:(j + 1) * cs, :]
        shift = 1
        while shift < cs:
            gc = gc + jnp.where(row >= shift, pltpu.roll(gc, shift, axis=0), 0.0)
            shift *= 2
        gcs.append(gc)
        gc_ts.append(gc.T)

    items = [(j, hd) for j in range(n_sub) for hd in range(GDN_HEADS)]
    idx = range(len(items))
    rows = [slice(j * cs, (j + 1) * cs) for j, _ in items]
    q = [qkv_ref[rows[i], hd * GDN_DK:(hd + 1) * GDN_DK] for i, (j, hd) in enumerate(items)]
    k = [qkv_ref[rows[i], hk + hd * GDN_DK:hk + (hd + 1) * GDN_DK] for i, (j, hd) in enumerate(items)]
    v = [qkv_ref[rows[i], 2 * hk + hd * GDN_DV:2 * hk + (hd + 1) * GDN_DV] for i, (j, hd) in enumerate(items)]
    beta = [bg_ref[rows[i], hd:hd + 1] for i, (j, hd) in enumerate(items)]
    g_col = [gcs[j][:, GDN_HEADS + hd:GDN_HEADS + hd + 1] for j, hd in items]
    g_row = [gc_ts[j][GDN_HEADS + hd:GDN_HEADS + hd + 1, :] for j, hd in items]
    g_last = [gcs[j][cs - 1:cs, GDN_HEADS + hd:GDN_HEADS + hd + 1] for j, hd in items]
    decay = [jnp.exp(jnp.where(lower, g_col[i] - g_row[i], -jnp.inf)) for i in idx]
    eg = [jnp.exp(g_col[i]) for i in idx]
    kb = [k[i] * beta[i] for i in idx]

    kq = [_dot_nt(jnp.concatenate([kb[i], q[i]], axis=0), k[i]) for i in idx]
    qkd = [kq[i][cs:] * decay[i] for i in idx]
    p = [-jnp.where(strict, kq[i][:cs] * decay[i], 0.0) for i in idx]
    t_inv = [eye + p[i] for i in idx]
    p = [_dot3(p[i], p[i]) for i in idx]
    span = 2
    while span < cs:
        m = [_dot3(p[i], jnp.concatenate([t_inv[i], p[i]], axis=1)) for i in idx]
        t_inv = [t_inv[i] + m[i][:, :cs] for i in idx]
        p = [m[i][:, cs:] for i in idx]
        span *= 2
    sol = [_dot3(t_inv[i], jnp.concatenate([v[i] * beta[i], kb[i] * eg[i]], axis=1)) for i in idx]
    qe = [q[i] * eg[i] for i in idx]
    kd = [k[i] * jnp.exp(g_last[i] - g_col[i]) for i in idx]
    e_last = [jnp.exp(g_last[i]) for i in idx]

    s = [s_ref[hd] for hd in range(GDN_HEADS)]
    for j in range(n_sub):
        sub = [j * GDN_HEADS + hd for hd in range(GDN_HEADS)]
        wq = [_dot(jnp.concatenate([sol[i][:, GDN_DV:], qe[i]], axis=0), s[hd]) for hd, i in enumerate(sub)]
        v_new = [sol[i][:, :GDN_DV] - wq[hd][:cs] for hd, i in enumerate(sub)]
        o_in = [_dot(qkd[i], v_new[hd]) for hd, i in enumerate(sub)]
        s_add = [_dot_tn(kd[i], v_new[hd]) for hd, i in enumerate(sub)]
        for hd, i in enumerate(sub):
            o_ref[rows[i], hd * GDN_DV:(hd + 1) * GDN_DV] = wq[hd][cs:] + o_in[hd]
        s = [s[hd] * e_last[i] + s_add[hd] for hd, i in enumerate(sub)]
    for hd in range(GDN_HEADS):
        s_ref[hd] = s[hd]

    @pl.when(c == n_c - 1)
    def _():
        s_out_ref[...] = s_ref[...]


def gdn_chunks(qkv, bg):
    nb, t_pad, _ = qkv.shape
    rows = GDN_CHUNK * GDN_CHUNKS_PER_STEP
    return pl.pallas_call(
        _gdn_chunk_kernel,
        grid=(nb, t_pad // rows),
        in_specs=[pl.BlockSpec((None, rows, GDN_QKV), lambda b, c: (b, c, 0)),
                  pl.BlockSpec((None, rows, LANES), lambda b, c: (b, c, 0))],
        out_specs=[pl.BlockSpec((None, rows, GDN_Z), lambda b, c: (b, c, 0)),
                   pl.BlockSpec((None, GDN_HEADS, GDN_DK, GDN_DV), lambda b, c: (b, 0, 0, 0))],
        out_shape=[jax.ShapeDtypeStruct((nb, t_pad, GDN_Z), F32),
                   jax.ShapeDtypeStruct((nb, GDN_HEADS, GDN_DK, GDN_DV), F32)],
        scratch_shapes=[pltpu.VMEM((GDN_HEADS, GDN_DK, GDN_DV), F32)],
        compiler_params=_params("parallel", "arbitrary"),
        name="gdn_chunks",
    )(qkv, bg)


GDN_STEP_GROUP = 4


def _gdn_step_kernel(s_ref, qk_ref, v_ref, bg_ref, o_ref, s_out_ref, *, group):
    for i in range(group):
        qk_t = qk_ref[i].T
        for hd in range(GDN_HEADS):
            beta = bg_ref[i, :, hd:hd + 1]
            g = bg_ref[i, :, GDN_HEADS + hd:GDN_HEADS + hd + 1]
            q_col = qk_t[:, hd:hd + 1]
            k_col = qk_t[:, GDN_HEADS + hd:GDN_HEADS + hd + 1]
            v_row = v_ref[i, :, hd * GDN_DV:(hd + 1) * GDN_DV]
            s = s_ref[i, hd] * jnp.exp(g)
            ks = jnp.sum(s * k_col, axis=0, keepdims=True)
            delta = (v_row - ks) * beta
            s = s + k_col * delta
            s_out_ref[i, hd] = s
            o_ref[i, :, hd * GDN_DV:(hd + 1) * GDN_DV] = jnp.sum(s * q_col, axis=0, keepdims=True)


def gdn_step(s0, qk, v, bg):
    n = s0.shape[0]
    group = math.gcd(n, GDN_STEP_GROUP)
    s_spec = pl.BlockSpec((group, GDN_HEADS, GDN_DK, GDN_DV), lambda i: (i, 0, 0, 0))
    return pl.pallas_call(
        functools.partial(_gdn_step_kernel, group=group),
        grid=(n // group,),
        in_specs=[s_spec,
                  pl.BlockSpec((group, 2 * GDN_HEADS, GDN_DK), lambda i: (i, 0, 0)),
                  pl.BlockSpec((group, 1, GDN_Z), lambda i: (i, 0, 0)),
                  pl.BlockSpec((group, 1, LANES), lambda i: (i, 0, 0))],
        out_specs=[pl.BlockSpec((group, 1, GDN_Z), lambda i: (i, 0, 0)), s_spec],
        out_shape=[jax.ShapeDtypeStruct((n, 1, GDN_Z), F32),
                   jax.ShapeDtypeStruct((n, GDN_HEADS, GDN_DK, GDN_DV), F32)],
        compiler_params=_params("parallel"),
        name="gdn_step",
    )(s0, qk, v, bg)


SB_HEAD_GROUP = 4


def _sb_prompt_kernel(bias_ref, q_ref, k_ref, v_ref, o_ref):
    hg = pl.program_id(1)
    qi = pl.program_id(2)
    tq, tk, dh = SB_TQ, SB_TK, SB_DH
    n_sub = tq // tk
    heads = range(SB_HEAD_GROUP)
    bias = [bias_ref[hg * SB_HEAD_GROUP + h] * LOG2_E for h in heads]
    q = [(q_ref[:, h * dh:(h + 1) * dh] * (dh ** -0.5 * LOG2_E)).astype(BF16) for h in heads]
    tr = _iota((2 * tk, 2 * tk), 0) & (tk - 1)
    tc = _iota((2 * tk, 2 * tk), 1)
    tail = jnp.where((tc >= tk) | (tr > tc), -1.0, 0.0).astype(BF16)
    below_diag = _iota((tq, tq), 1) < _iota((tq, tq), 0)

    def tile(k0, masked, acc, run):
        kblk = [k_ref[pl.ds(k0, tq), h * dh:(h + 1) * dh].astype(BF16) for h in heads]
        vblk = [v_ref[pl.ds(k0, tq), h * dh:(h + 1) * dh].astype(BF16) for h in heads]
        z = [lax.dot_general(q[h], kblk[h], (((1,), (1,)), ((), ())), preferred_element_type=F32) + bias[h]
             for h in heads]
        z_pos = [jnp.maximum(z[h], 0.0) for h in heads]
        z_neg = [jnp.minimum(z[h], 0.0) for h in heads]
        lg = [jnp.log(1.0 + jnp.exp2(z_neg[h] - z_pos[h])) * LOG2_E for h in heads]
        sp = [z_pos[h] + lg[h] for h in heads]
        log_beta = [z_neg[h] - lg[h] for h in heads]
        if masked:
            sp = [jnp.where(below_diag, sp[h], 0.0) for h in heads]
        split = [_split(sp[h]) for h in heads]
        res = [[jnp.dot(jnp.concatenate([part[:, s * tk:(s + 1) * tk] for part in split[h]], axis=1),
                        tail, preferred_element_type=F32) for s in range(n_sub)] for h in heads]
        acc, run = list(acc), list(run)
        for h in heads:
            w = [None] * n_sub
            for s in reversed(range(n_sub)):
                w[s] = jnp.exp2(log_beta[h][:, s * tk:(s + 1) * tk] + (res[h][s][:, :tk] + run[h]))
                run[h] = run[h] + res[h][s][:, tk:]
            w = jnp.concatenate(w, axis=1)
            if masked:
                w = jnp.where(below_diag, w, 0.0)
            acc[h] = acc[h] + jnp.dot(w.astype(BF16), vblk[h], preferred_element_type=F32)
        return tuple(acc), tuple(run)

    zeros = lambda width: tuple(jnp.zeros((tq, width), F32) for _ in heads)
    acc, run = tile(pl.multiple_of(qi * tq, tq), True, zeros(dh), zeros(tk))

    def body(i, carry):
        return tile(pl.multiple_of((qi - 1 - i) * tq, tq), False, *carry)

    acc, _ = lax.fori_loop(0, qi, body, (acc, run))
    for h in heads:
        o_ref[:, h * dh:(h + 1) * dh] = acc[h]


def sb_prompt_attn(qkv, bias):
    nb, t_pad, _ = qkv.shape
    tq = SB_TQ
    gw = SB_HEAD_GROUP * SB_DH
    n_groups = SB_HEADS // SB_HEAD_GROUP
    grid_spec = pltpu.PrefetchScalarGridSpec(
        num_scalar_prefetch=0,
        grid=(nb, n_groups, t_pad // tq),
        in_specs=[pl.BlockSpec(memory_space=pltpu.SMEM),
                  pl.BlockSpec((None, tq, gw), lambda b, g, i: (b, i, g)),
                  pl.BlockSpec((None, t_pad, gw), lambda b, g, i: (b, 0, n_groups + g)),
                  pl.BlockSpec((None, t_pad, gw), lambda b, g, i: (b, 0, 2 * n_groups + g))],
        out_specs=pl.BlockSpec((None, tq, gw), lambda b, g, i: (b, i, g)),
    )
    return pl.pallas_call(
        _sb_prompt_kernel,
        grid_spec=grid_spec,
        out_shape=jax.ShapeDtypeStruct((nb, t_pad, D_MODEL), F32),
        compiler_params=_params("parallel", "parallel", "arbitrary"),
        name="sb_prompt_attn",
    )(bias, qkv, qkv, qkv)


SB_PAGES_PER_STEP = 8


def _sb_decode_kernel(pt_ref, q_ref, bias_ref, expand_ref, *refs, n_pg):
    k_refs, v_refs = refs[:n_pg], refs[n_pg:2 * n_pg]
    o_ref, acc_ref, run_ref = refs[2 * n_pg:]
    step = pl.program_id(1)
    n_steps = pl.num_programs(1)
    ps = PAGE_SIZE
    pages = range(n_pg)

    @pl.when(step == 0)
    def _():
        acc_ref[...] = jnp.zeros_like(acc_ref)
        run_ref[...] = jnp.zeros_like(run_ref)

    q_mat = q_ref[...]
    lane = _iota((ps, LANES), 1)
    z = []
    for i in pages:
        zi = jnp.zeros((ps, LANES), F32)
        for hd in range(SB_HEADS):
            k_head = k_refs[i][pl.ds(hd, ps, stride=SB_HEADS), :]
            zh = jnp.dot(k_head.astype(BF16), q_mat, preferred_element_type=F32)
            zi = jnp.where(lane == hd, zh, zi)
        z.append(zi * SB_DH ** -0.5 + bias_ref[...])
    lg = [jnp.log(1.0 + jnp.exp(-jnp.abs(z[i]))) for i in pages]
    sp = [jnp.maximum(z[i], 0.0) + lg[i] for i in pages]
    log_beta = [jnp.minimum(z[i], 0.0) - lg[i] for i in pages]
    nr = _iota((ps, 2 * ps), 0)
    nc = _iota((ps, 2 * ps), 1) & (ps - 1)
    newer = jnp.where(nc > nr, -1.0, 0.0).astype(BF16)
    after = [jnp.dot(newer, jnp.concatenate(_split(sp[i]), axis=0), preferred_element_type=F32) for i in pages]
    run = run_ref[...]
    w = []
    for i in pages:
        w.append(jnp.exp(log_beta[i] + after[i] + run).astype(BF16))
        run = run + (after[i][0:1, :] - sp[i][0:1, :])
    run_ref[...] = run
    w_wide = [jnp.dot(w[i], expand_ref[...], preferred_element_type=F32) for i in pages]
    for hd in range(SB_HEADS):
        part = None
        for i in pages:
            v_head = v_refs[i][pl.ds(hd, ps, stride=SB_HEADS), :]
            prod = w_wide[i][:, hd * SB_DH:(hd + 1) * SB_DH] * v_head
            prod = jnp.sum(prod.reshape(ps // SUBLANES, SUBLANES, SB_DH), axis=0)
            part = prod if part is None else part + prod
        acc_ref[hd] += part

    @pl.when(step == n_steps - 1)
    def _():
        for hd in range(SB_HEADS):
            o_ref[:, hd * SB_DH:(hd + 1) * SB_DH] = jnp.sum(acc_ref[hd], axis=0, keepdims=True)


def sb_decode_attn(q_mat, bias_row, expand, cache_k, cache_v, page_table):
    n, n_pages = page_table.shape
    n_pg = math.gcd(n_pages, SB_PAGES_PER_STEP)

    def page_spec(i):
        return pl.BlockSpec((None, PAGE_SIZE * SB_HEADS, SB_DH),
                            lambda s, p, pt: (pt[s, n_pages - 1 - (p * n_pg + i)], 0, 0))

    grid_spec = pltpu.PrefetchScalarGridSpec(
        num_scalar_prefetch=1,
        grid=(n, n_pages // n_pg),
        in_specs=[pl.BlockSpec((None, SB_DH, LANES), lambda s, p, pt: (s, 0, 0)),
                  pl.BlockSpec((1, LANES), lambda s, p, pt: (0, 0)),
                  pl.BlockSpec((LANES, D_MODEL), lambda s, p, pt: (0, 0))]
                 + [page_spec(i) for i in range(n_pg)] * 2,
        out_specs=pl.BlockSpec((None, 1, D_MODEL), lambda s, p, pt: (s, 0, 0)),
        scratch_shapes=[pltpu.VMEM((SB_HEADS, SUBLANES, SB_DH), F32), pltpu.VMEM((1, LANES), F32)],
    )
    return pl.pallas_call(
        functools.partial(_sb_decode_kernel, n_pg=n_pg),
        grid_spec=grid_spec,
        out_shape=jax.ShapeDtypeStruct((n, 1, D_MODEL), F32),
        compiler_params=_params("parallel", "arbitrary"),
        name="sb_decode_attn",
    )(page_table, q_mat, bias_row, expand, *([cache_k] * n_pg), *([cache_v] * n_pg))


def _row(v):
    return v.reshape(1, -1).astype(F32)


def _pad_lanes(v, offset):
    out = jnp.zeros((1, LANES), F32)
    return lax.dynamic_update_slice(out, v.reshape(1, -1).astype(F32), (0, offset))


def kernel(x_prompt, x_sample, state_gdn_conv, state_gdn_S, cache_sb_k, cache_sb_v, state_conv, page_table,
           meta_tokens, ln_g, ln_b, ffn_w_gate, ffn_w_up, ffn_w_down,
           gdn_w_in, gdn_conv_w, gdn_a_log, gdn_dt_bias, gdn_norm_w, gdn_w_out,
           sb_w_qkv, sb_w_out, sb_logit_bias,
           cv_w_pw1, cv_b_pw1, cv_dw_w, cv_dw_b, cv_ln_g, cv_ln_b, cv_w_pw2, cv_b_pw2):
    nb, seq, _ = x_prompt.shape
    ns = x_sample.shape[0]
    assert x_sample.shape[1] == 1, "the sample group holds one new token per sequence"
    t_real = N_META + seq
    t_pad = -(-t_real // SEQ_ALIGN) * SEQ_ALIGN
    rows_p = nb * t_pad

    meta = jnp.broadcast_to(meta_tokens.astype(F32)[None], (nb, N_META, D_MODEL))
    hp = jnp.concatenate([meta, x_prompt, jnp.zeros((nb, t_pad - t_real, D_MODEL), F32)], axis=1)
    hp = hp.reshape(rows_p, D_MODEL)
    hs = x_sample.reshape(ns, D_MODEL)

    gdn_s_p, gdn_s_s, gdn_c_p, gdn_c_s = [], [], [], []
    sb_k_p, sb_v_p, sb_k_s, sb_v_s = [], [], [], []
    cv_p, cv_s = [], []
    zero_bias = jnp.zeros((1, D_MODEL), F32)

    def ffn(h, i, half, ln_idx):
        return ffn_ln(h, ffn_w_gate[i, half].astype(BF16), ffn_w_up[i, half].astype(BF16),
                      ffn_w_down[i, half].astype(BF16), _row(ln_g[i, ln_idx]), _row(ln_b[i, ln_idx]))

    for i in range(DEPTH):
        j, kind = i // N_MIXERS, i % N_MIXERS
        hp = ffn(hp, i, 0, 0)
        hs = ffn(hs, i, 0, 0)
        g1, b1 = _row(ln_g[i, 1]), _row(ln_b[i, 1])

        if kind == 0:
            w_in = gdn_w_in[j]
            w_qkv = w_in[:, :GDN_QKV].astype(BF16)
            w_z = w_in[:, GDN_QKV:GDN_QKV + GDN_Z].astype(BF16)
            w_ba = jnp.pad(w_in[:, GDN_QKV + GDN_Z:], ((0, 0), (0, LANES - 2 * GDN_HEADS))).astype(BF16)
            conv_w = gdn_conv_w[j].astype(F32)
            a_log_row = _pad_lanes(gdn_a_log[j], GDN_HEADS)
            dt_bias_row = _pad_lanes(gdn_dt_bias[j], GDN_HEADS)
            norm_w = _row(gdn_norm_w[j])
            w_out = gdn_w_out[j].astype(BF16)

            qkv_pre, z, ba = proj(hp, [w_qkv, w_z, w_ba])
            qkv_pre3 = qkv_pre.reshape(nb, t_pad, GDN_QKV)
            qkv_act, bg = gdn_act(qkv_pre3, ba.reshape(nb, t_pad, LANES), conv_w, a_log_row, dt_bias_row, t_real)
            o, s_fin = gdn_chunks(qkv_act, bg)
            hp = gdn_out_ln(o.reshape(rows_p, GDN_Z), z, hp, norm_w, w_out, g1, b1)
            gdn_s_p.append(s_fin)
            gdn_c_p.append(qkv_pre3[:, t_real - (GDN_CONV - 1):t_real])

            qkv_pre, z, ba = proj(hs, [w_qkv, w_z, w_ba])
            seq_s = jnp.concatenate([state_gdn_conv[j].astype(F32), qkv_pre[:, None]], axis=1)
            qkv_act, bg = gdn_act_sample(jnp.swapaxes(seq_s, 0, 1), ba, conv_w, a_log_row, dt_bias_row)
            hk = GDN_HEADS * GDN_DK
            qk_rows = qkv_act[:, :2 * hk].reshape(ns, 2 * GDN_HEADS, GDN_DK)
            v_row = qkv_act[:, 2 * hk:].reshape(ns, 1, GDN_Z)
            o, s_fin = gdn_step(state_gdn_S[j].astype(F32), qk_rows, v_row, bg.reshape(ns, 1, LANES))
            hs = gdn_out_ln(o.reshape(ns, GDN_Z), z, hs, norm_w, w_out, g1, b1)
            gdn_s_s.append(s_fin)
            gdn_c_s.append(seq_s[:, 1:])

        elif kind == 1:
            w_qkv = sb_w_qkv[j].astype(BF16)
            w_out = sb_w_out[j].astype(BF16)
            bias = sb_logit_bias[j].astype(F32)

            (qkv,) = proj(hp, [w_qkv])
            qkv3 = qkv.reshape(nb, t_pad, 3 * D_MODEL)
            o = sb_prompt_attn(qkv3, bias)
            hp = proj_res_ln(o.reshape(rows_p, D_MODEL), hp, w_out, zero_bias, g1, b1)
            sb_k_p.append(qkv3[:, :t_real, D_MODEL:2 * D_MODEL].reshape(nb, t_real, SB_HEADS, SB_DH))
            sb_v_p.append(qkv3[:, :t_real, 2 * D_MODEL:].reshape(nb, t_real, SB_HEADS, SB_DH))

            (qkv,) = proj(hs, [w_qkv])
            q_t = jnp.swapaxes(qkv[:, :D_MODEL].reshape(ns, SB_HEADS, SB_DH), 1, 2)
            q_mat = jnp.pad(q_t, ((0, 0), (0, 0), (0, LANES - SB_HEADS))).astype(BF16)
            head_of_lane = jnp.arange(D_MODEL) // SB_DH
            expand = (jnp.arange(LANES)[:, None] == head_of_lane[None, :]).astype(BF16)
            bias_row = _pad_lanes(bias, 0)
            n_phys = cache_sb_k.shape[1]
            o = sb_decode_attn(q_mat, bias_row, expand,
                               cache_sb_k.reshape(-1, PAGE_SIZE * SB_HEADS, SB_DH).astype(F32),
                               cache_sb_v.reshape(-1, PAGE_SIZE * SB_HEADS, SB_DH).astype(F32),
                               page_table.astype(jnp.int32) + j * n_phys)
            hs = proj_res_ln(o.reshape(ns, D_MODEL), hs, w_out, zero_bias, g1, b1)
            sb_k_s.append(qkv[:, D_MODEL:2 * D_MODEL].reshape(ns, 1, SB_HEADS, SB_DH))
            sb_v_s.append(qkv[:, 2 * D_MODEL:].reshape(ns, 1, SB_HEADS, SB_DH))

        else:
            w1 = cv_w_pw1[j].astype(BF16)
            b_pw1 = _row(cv_b_pw1[j])
            dw_w = cv_dw_w[j].astype(F32)
            dw_b, cg, cb = _row(cv_dw_b[j]), _row(cv_ln_g[j]), _row(cv_ln_b[j])
            w2 = cv_w_pw2[j].astype(BF16)
            b_pw2 = _row(cv_b_pw2[j])

            u = glu_proj(hp, w1, b_pw1).reshape(nb, t_pad, D_MODEL)
            d = cv_conv(u, dw_w, dw_b, cg, cb)
            hp = proj_res_ln(d.reshape(rows_p, D_MODEL), hp, w2, b_pw2, g1, b1)
            cv_p.append(u[:, t_real - (CONV_WIDTH - 1):t_real])

            u = glu_proj(hs, w1, b_pw1)
            seq_s = jnp.concatenate([state_conv[j].astype(F32), u[:, None]], axis=1)
            d = cv_conv_sample(jnp.swapaxes(seq_s, 0, 1), dw_w, dw_b, cg, cb)
            hs = proj_res_ln(d, hs, w2, b_pw2, g1, b1)
            cv_s.append(seq_s[:, 1:])

        hp = ffn(hp, i, 1, 2)
        hs = ffn(hs, i, 1, 2)

    y_prompt = hp.reshape(nb, t_pad, D_MODEL)[:, N_META:t_real]
    y_sample = hs.reshape(ns, 1, D_MODEL)
    return (y_prompt, y_sample,
            jnp.stack(gdn_s_p), jnp.stack(gdn_s_s), jnp.stack(gdn_c_p), jnp.stack(gdn_c_s),
            jnp.stack(sb_k_p), jnp.stack(sb_v_p), jnp.stack(sb_k_s), jnp.stack(sb_v_s),
            jnp.stack(cv_p), jnp.stack(cv_s))
```
